```python
import math
import jax
import jax.numpy as jnp
from jax import lax
import numpy as np

D_MODEL = 2048
BATCH = 1
SEQ = 8192
DEPTH = 1

CHUNK = 64
NORM_EPS = 1e-6
RWKV_HEAD_DIM = 64
RWKV_HEADS = D_MODEL // RWKV_HEAD_DIM
RWKV_DECAY_LORA = 96
RWKV_AAA_LORA = 96
RWKV_GATE_LORA = 256
RWKV_GN_EPS = 64e-5
GDN_HEAD_DIM = 128
GDN_HEADS = D_MODEL // GDN_HEAD_DIM
GDN_CONV = 4
D_FF = -(-8 * D_MODEL // (3 * 256)) * 256

RWKV_SPLITS = (D_MODEL, D_MODEL, D_MODEL, RWKV_DECAY_LORA, RWKV_AAA_LORA, RWKV_GATE_LORA)
RWKV_COLS = sum(RWKV_SPLITS)
IN_SPLITS = (RWKV_COLS, 3 * D_MODEL, D_MODEL, GDN_HEADS, GDN_HEADS, D_MODEL, D_MODEL)
IN_COLS = sum(IN_SPLITS)

kernel_name = 'hybrid_rwkv7_gdn_sandwich_block'


def _split(t, sizes):
    return jnp.split(t, [int(s) for s in np.cumsum(sizes)[:-1]], axis=-1)


def rmsnorm(x, w, eps=NORM_EPS):
    xf = x.astype(jnp.float32)
    y = xf * lax.rsqrt(jnp.mean(xf * xf, axis=-1, keepdims=True) + eps)
    return (y * w.astype(jnp.float32)).astype(x.dtype)


def l2norm(x, eps=1e-6):
    xf = x.astype(jnp.float32)
    return xf * lax.rsqrt(jnp.sum(xf * xf, axis=-1, keepdims=True) + eps)


def token_shift(x):
    return jnp.pad(x, ((0, 0), (1, 0), (0, 0)))[:, :-1]


def causal_depthwise_conv(x, w):
    k_width, chans = w.shape
    return lax.conv_general_dilated(
        x, w[:, None, :].astype(x.dtype), window_strides=(1,),
        padding=[(k_width - 1, 0)], dimension_numbers=('NWC', 'WIO', 'NWC'),
        feature_group_count=chans)


def rwkv7_recurrence(r, w, k, v, a, b):
    bsz, _, heads, n = r.shape

    def step(state, inp):
        r_t, w_t, k_t, v_t, a_t, b_t = inp
        sa = jnp.einsum('bhvk,bhk->bhv', state, a_t)
        state = (state * w_t[:, :, None, :] + sa[..., None] * b_t[:, :, None, :]
                 + v_t[..., None] * k_t[:, :, None, :])
        return state, jnp.einsum('bhvk,bhk->bhv', state, r_t)

    xs = tuple(jnp.moveaxis(t, 1, 0) for t in (r, w, k, v, a, b))
    s0 = jnp.zeros((bsz, heads, n, n), jnp.float32)
    _, y = lax.scan(step, s0, xs)
    return jnp.moveaxis(y, 0, 1)


def gated_delta_rule_chunked(q, k, v, g, beta):
    bsz, seq, heads, dk = q.shape
    dv = v.shape[-1]
    n_chunks = seq // CHUNK

    def blocks(t):
        t = t.reshape((bsz, n_chunks, CHUNK, heads) + t.shape[3:])
        return jnp.moveaxis(t, 3, 1)

    q, k, v, g, beta = blocks(q), blocks(k), blocks(v), blocks(g), blocks(beta)
    gc = jnp.cumsum(g, axis=-1)
    causal = jnp.tril(jnp.ones((CHUNK, CHUNK), bool))
    strict = jnp.tril(jnp.ones((CHUNK, CHUNK), bool), -1)
    decay = jnp.exp(jnp.where(causal, gc[..., :, None] - gc[..., None, :], -jnp.inf))
    kb = k * beta[..., None]
    vb = v * beta[..., None]
    lower = jnp.where(strict, jnp.einsum('bhncd,bhnsd->bhncs', kb, k) * decay, 0.0)
    eye = jnp.eye(CHUNK, dtype=q.dtype)
    tmat = lax.linalg.triangular_solve(eye + lower, jnp.broadcast_to(eye, lower.shape),
                                       left_side=True, lower=True)
    u = tmat @ vb
    wk = tmat @ (kb * jnp.exp(gc)[..., None])
    qk = jnp.einsum('bhncd,bhnsd->bhncs', q, k) * decay
    q_dec = q * jnp.exp(gc)[..., None]
    k_dec = k * jnp.exp(gc[..., -1:] - gc)[..., None]
    g_last = jnp.exp(gc[..., -1])

    def step(state, inp):
        u_i, w_i, qk_i, qd_i, kd_i, gl_i = inp
        v_new = u_i - w_i @ state
        o_i = qd_i @ state + qk_i @ v_new
        state = state * gl_i[..., None, None] + jnp.swapaxes(kd_i, -1, -2) @ v_new
        return state, o_i

    xs = tuple(jnp.moveaxis(t, 2, 0) for t in (u, wk, qk, q_dec, k_dec, g_last))
    s0 = jnp.zeros((bsz, heads, dk, dv), jnp.float32)
    _, o = lax.scan(step, s0, xs)
    o = jnp.moveaxis(o, 0, 2)
    return jnp.moveaxis(o, 1, 3).reshape(bsz, seq, heads, dv)


def hybrid_layer(x, norm_mix_pre, norm_mix_post, norm_ffn_pre, norm_ffn_post, w_in,
                 rwkv_mix, rwkv_w0, rwkv_w2, rwkv_a0, rwkv_a2, rwkv_g2, rwkv_k_k, rwkv_k_a,
                 rwkv_r_k, rwkv_ln_w, rwkv_ln_b, gdn_conv_w, gdn_a_log, gdn_dt_bias,
                 gdn_norm_w, w_out, ffn_w_gate, ffn_w_up, ffn_w_down):
    bsz, seq, d = x.shape
    f32 = jnp.float32
    rwkv_heads = lambda t: t.reshape(bsz, seq, RWKV_HEADS, RWKV_HEAD_DIM)
    gdn_heads = lambda t: t.reshape(bsz, seq, GDN_HEADS, GDN_HEAD_DIM)

    xn = rmsnorm(x, norm_mix_pre)
    p = (xn @ w_in).astype(f32)
    p_rwkv, p_qkv, p_z, p_beta, p_alpha, p_gate_a, p_gate_b = _split(p, IN_SPLITS)

    p_rwkv = p_rwkv + (token_shift(p_rwkv) - p_rwkv) * rwkv_mix.astype(f32)
    r, k, v, lw, la, lg = _split(p_rwkv, RWKV_SPLITS)
    w_log = -jax.nn.softplus(-(rwkv_w0.astype(f32) + jnp.tanh(lw) @ rwkv_w2.astype(f32))) - 0.5
    decay = jnp.exp(-jnp.exp(w_log))
    a = jax.nn.sigmoid(rwkv_a0.astype(f32) + la @ rwkv_a2.astype(f32))
    out_gate = jax.nn.sigmoid(lg) @ rwkv_g2.astype(f32)
    kk = l2norm(rwkv_heads(k * rwkv_k_k.astype(f32)))
    k = k * (1.0 + (a - 1.0) * rwkv_k_a.astype(f32))
    rh, kh, vh, ah = rwkv_heads(r), rwkv_heads(k), rwkv_heads(v), rwkv_heads(a)
    y = rwkv7_recurrence(rh, rwkv_heads(decay), kh, vh, -kk, kk * ah)
    mu = jnp.mean(y, axis=-1, keepdims=True)
    var = jnp.mean(jnp.square(y - mu), axis=-1, keepdims=True)
    y = ((y - mu) * lax.rsqrt(var + RWKV_GN_EPS)).reshape(bsz, seq, d) * rwkv_ln_w.astype(f32) + rwkv_ln_b.astype(f32)
    bonus = jnp.sum(rh * kh * rwkv_r_k.astype(f32), axis=-1, keepdims=True) * vh
    y_a = (y + bonus.reshape(bsz, seq, d)) * out_gate

    qkv = jax.nn.silu(causal_depthwise_conv(p_qkv, gdn_conv_w.astype(f32)))
    q, k, v = _split(qkv, (d, d, d))
    q = l2norm(gdn_heads(q)) * GDN_HEAD_DIM ** -0.5
    k = l2norm(gdn_heads(k))
    beta = jax.nn.sigmoid(p_beta)
    g = -jnp.exp(gdn_a_log.astype(f32)) * jax.nn.softplus(p_alpha + gdn_dt_bias.astype(f32))
    o = gated_delta_rule_chunked(q, k, gdn_heads(v), g, beta)
    o = rmsnorm(o, gdn_norm_w) * jax.nn.silu(gdn_heads(p_z))
    y_b = o.reshape(bsz, seq, d)

    mixed = jax.nn.sigmoid(p_gate_a) * y_a + jax.nn.sigmoid(p_gate_b) * y_b
    x = x + rmsnorm(mixed.astype(x.dtype) @ w_out, norm_mix_post)

    xn = rmsnorm(x, norm_ffn_pre)
    hidden = jax.nn.silu(xn @ ffn_w_gate) * (xn @ ffn_w_up)
    return x + rmsnorm(hidden @ ffn_w_down, norm_ffn_post)


def setup_inputs(seed: int = 0) -> dict:
    key = jax.random.key(seed)
    ks = jax.random.split(key, 26)
    L, D = DEPTH, D_MODEL
    nrm = lambda kk, shape, s: s * jax.random.normal(kk, shape, jnp.float32)
    gain = lambda kk, shape: 1.0 + nrm(kk, shape, 0.05)
    dt = jnp.exp(jax.random.uniform(ks[21], (L, GDN_HEADS), jnp.float32, math.log(1e-3), math.log(1e-1)))
    return {
        'x': nrm(ks[0], (BATCH, SEQ, D), 1.0),
        'norm_mix_pre': gain(ks[1], (L, D)),
        'norm_mix_post': gain(ks[2], (L, D)),
        'norm_ffn_pre': gain(ks[3], (L, D)),
        'norm_ffn_post': gain(ks[4], (L, D)),
        'w_in': nrm(ks[5], (L, D, IN_COLS), D ** -0.5),
        'rwkv_mix': jax.random.uniform(ks[6], (L, RWKV_COLS), jnp.float32),
        'rwkv_w0': jnp.linspace(-5.0, 1.0, D, dtype=jnp.float32)[None, :] + nrm(ks[7], (L, D), 0.1),
        'rwkv_w2': nrm(ks[8], (L, RWKV_DECAY_LORA, D), 0.1 * RWKV_DECAY_LORA ** -0.5),
        'rwkv_a0': nrm(ks[9], (L, D), 0.1),
        'rwkv_a2': nrm(ks[10], (L, RWKV_AAA_LORA, D), 0.1 * RWKV_AAA_LORA ** -0.5),
        'rwkv_g2': nrm(ks[11], (L, RWKV_GATE_LORA, D), RWKV_GATE_LORA ** -0.5),
        'rwkv_k_k': 0.85 + nrm(ks[12], (L, D), 0.05),
        'rwkv_k_a': gain(ks[13], (L, D)),
        'rwkv_r_k': nrm(ks[14], (L, RWKV_HEADS, RWKV_HEAD_DIM), 0.1),
        'rwkv_ln_w': gain(ks[15], (L, D)),
        'rwkv_ln_b': nrm(ks[16], (L, D), 0.02),
        'gdn_conv_w': nrm(ks[17], (L, GDN_CONV, 3 * D), GDN_CONV ** -0.5),
        'gdn_a_log': jnp.log(jax.random.uniform(ks[18], (L, GDN_HEADS), jnp.float32, 1.0, 16.0)),
        'gdn_dt_bias': dt + jnp.log(-jnp.expm1(-dt)),
        'gdn_norm_w': gain(ks[19], (L, GDN_HEAD_DIM)),
        'w_out': nrm(ks[20], (L, D, D), D ** -0.5),
        'ffn_w_gate': nrm(ks[22], (L, D, D_FF), D ** -0.5),
        'ffn_w_up': nrm(ks[23], (L, D, D_FF), D ** -0.5),
        'ffn_w_down': nrm(ks[24], (L, D_FF, D), D_FF ** -0.5),
    }


def reference(x, norm_mix_pre, norm_mix_post, norm_ffn_pre, norm_ffn_post, w_in,
              rwkv_mix, rwkv_w0, rwkv_w2, rwkv_a0, rwkv_a2, rwkv_g2, rwkv_k_k, rwkv_k_a,
              rwkv_r_k, rwkv_ln_w, rwkv_ln_b, gdn_conv_w, gdn_a_log, gdn_dt_bias,
              gdn_norm_w, w_out, ffn_w_gate, ffn_w_up, ffn_w_down):
    h = x
    for l in range(DEPTH):
        h = hybrid_layer(h, norm_mix_pre[l], norm_mix_post[l], norm_ffn_pre[l], norm_ffn_post[l],
                         w_in[l], rwkv_mix[l], rwkv_w0[l], rwkv_w2[l], rwkv_a0[l], rwkv_a2[l],
                         rwkv_g2[l], rwkv_k_k[l], rwkv_k_a[l], rwkv_r_k[l], rwkv_ln_w[l],
                         rwkv_ln_b[l], gdn_conv_w[l], gdn_a_log[l], gdn_dt_bias[l],
                         gdn_norm_w[l], w_out[l], ffn_w_gate[l], ffn_w_up[l], ffn_w_down[l])
    return h
```

```python
import functools

import jax
import jax.numpy as jnp
import numpy as np
from jax import lax
from jax.experimental import pallas as pl
from jax.experimental.pallas import tpu as pltpu

F32 = jnp.float32
BF16 = jnp.bfloat16
HI = lax.Precision.HIGHEST

D = 2048
SEQ = 8192
CHUNK = 64
NORM_EPS = 1e-6
RWKV_N = 64
RWKV_HEADS = D // RWKV_N
RWKV_GN_EPS = 64e-5
GDN_N = 128
GDN_HEADS = D // GDN_N
GDN_CONV = 4
D_FF = 5632
LANES = 128
SUBLANES = 8
VMEM_LIMIT = 56 * 1024 * 1024

C_R, C_K, C_V = 0, 2048, 4096
C_QKV = 6144
C_Z = 12288
C_GA = 14336
C_GB = 16384
C_LORA = 18432
C_BA = 18944
P_COLS = 19456
LORA_W = 512


def _dot(a, b, precision=None):
    return jnp.dot(a, b, preferred_element_type=F32, precision=precision)


def _dot_nt(a, b, precision=None):
    return lax.dot_general(a, b, (((1,), (1,)), ((), ())), preferred_element_type=F32,
                           precision=precision)


def _dot_tn(a, b, precision=None):
    return lax.dot_general(a, b, (((0,), (0,)), ((), ())), preferred_element_type=F32,
                           precision=precision)


def _iota2(shape, dim):
    return lax.broadcasted_iota(jnp.int32, shape, dim)


def _unit_lower_inverse(low, n_block):
    n = low.shape[0]
    row = _iota2((n, n), 0)
    col = _iota2((n, n), 1)
    x = jnp.where(row == col, 1.0, 0.0).astype(F32)
    b = 1
    while b < n_block:
        sel = ((row // (2 * b)) == (col // (2 * b))) & ((row // b) % 2 == 1) & ((col // b) % 2 == 0)
        c = jnp.where(sel, low, 0.0)
        if b == 1:
            x = x - c
        else:
            x = x - _dot(_dot(x, c, HI), x, HI)
        b *= 2
    return x


def _in_proj_kernel(x_ref, g_ref, w_ref, o_ref, xn_ref):
    @pl.when(pl.program_id(1) == 0)
    def _():
        x = x_ref[...]
        ms = jnp.mean(x * x, axis=-1, keepdims=True)
        xn_ref[...] = (x * lax.rsqrt(ms + NORM_EPS) * g_ref[...]).astype(BF16)

    o_ref[...] = _dot(xn_ref[...], w_ref[...])


def _in_proj(x, gain, w_all, tm=1024, tn=512):
    t = x.shape[0]
    return pl.pallas_call(
        _in_proj_kernel,
        grid=(t // tm, P_COLS // tn),
        in_specs=[
            pl.BlockSpec((tm, D), lambda i, j: (i, 0)),
            pl.BlockSpec((1, D), lambda i, j: (0, 0)),
            pl.BlockSpec((D, tn), lambda i, j: (0, j)),
        ],
        out_specs=pl.BlockSpec((tm, tn), lambda i, j: (i, j)),
        out_shape=jax.ShapeDtypeStruct((t, P_COLS), F32),
        scratch_shapes=[pltpu.VMEM((tm, D), BF16)],
        compiler_params=pltpu.CompilerParams(
            dimension_semantics=("arbitrary", "arbitrary"), vmem_limit_bytes=VMEM_LIMIT),
        name="in_proj",
    )(x, gain, w_all)


def _rwkv_prep_kernel(p_ref, pl_ref, mix_ref, mixl_ref, w0_ref, w2_ref, a0_ref, a2_ref, g2_ref,
                      kk_ref, ka_ref, e_ref, et_ref,
                      r_ref, lw_ref, k_ref, v_ref, an_ref, b_ref, gate_ref,
                      buf_ref, bufl_ref):
    tm = p_ref.shape[0]

    @pl.when(pl.program_id(0) == 0)
    def _():
        buf_ref[0:SUBLANES, :] = jnp.zeros((SUBLANES, 3 * D), F32)
        bufl_ref[0:SUBLANES, :] = jnp.zeros((SUBLANES, LORA_W), F32)

    buf_ref[SUBLANES:SUBLANES + tm, :] = p_ref[...]
    bufl_ref[SUBLANES:SUBLANES + tm, :] = pl_ref[...]

    def shifted(ref, lo, hi, mix):
        cur = ref[SUBLANES:SUBLANES + tm, lo:hi]
        prev = ref[SUBLANES - 1:SUBLANES - 1 + tm, lo:hi]
        return cur + (prev - cur) * mix

    r = shifted(buf_ref, C_R, C_R + D, mix_ref[:, C_R:C_R + D])
    k = shifted(buf_ref, C_K, C_K + D, mix_ref[:, C_K:C_K + D])
    v = shifted(buf_ref, C_V, C_V + D, mix_ref[:, C_V:C_V + D])
    lora = shifted(bufl_ref, 0, LORA_W, mixl_ref[...])
    lw = lora[:, 0:128]
    la = lora[:, 128:256]
    lg = lora[:, 256:512]

    w_log = -jax.nn.softplus(-(w0_ref[...] + _dot(jnp.tanh(lw), w2_ref[...]))) - 0.5
    a = jax.nn.sigmoid(a0_ref[...] + _dot(la, a2_ref[...]))
    gate = _dot(jax.nn.sigmoid(lg), g2_ref[...])

    kk = k * kk_ref[...]
    ss = _dot(kk * kk, e_ref[...], HI)
    kk = kk * _dot(lax.rsqrt(ss + 1e-6), et_ref[...], HI)

    r_ref[...] = r
    lw_ref[...] = -jnp.exp(w_log)
    k_ref[...] = k * (1.0 + (a - 1.0) * ka_ref[...])
    v_ref[...] = v
    an_ref[...] = -kk
    b_ref[...] = kk * a
    gate_ref[...] = gate

    buf_ref[0:SUBLANES, :] = buf_ref[tm:tm + SUBLANES, :]
    bufl_ref[0:SUBLANES, :] = bufl_ref[tm:tm + SUBLANES, :]


def _rwkv_prep(p, mix, mixl, w0, w2, a0, a2, g2, k_k, k_a, e, et, tm=128):
    t = p.shape[0]
    row = lambda w: pl.BlockSpec((1, w), lambda i: (0, 0))
    full = lambda a: pl.BlockSpec(a.shape, lambda i: (0, 0))
    out = jax.ShapeDtypeStruct((t, D), F32)
    return pl.pallas_call(
        _rwkv_prep_kernel,
        grid=(t // tm,),
        in_specs=[
            pl.BlockSpec((tm, 3 * D), lambda i: (i, 0)),
            pl.BlockSpec((tm, LORA_W), lambda i: (i, C_LORA // LORA_W)),
            row(3 * D), row(LORA_W), row(D), full(w2), row(D), full(a2), full(g2),
            row(D), row(D), full(e), full(et),
        ],
        out_specs=[pl.BlockSpec((tm, D), lambda i: (i, 0))] * 7,
        out_shape=[out] * 7,
        scratch_shapes=[pltpu.VMEM((tm + SUBLANES, 3 * D), F32),
                        pltpu.VMEM((tm + SUBLANES, LORA_W), F32)],
        compiler_params=pltpu.CompilerParams(
            dimension_semantics=("arbitrary",), vmem_limit_bytes=VMEM_LIMIT),
        name="rwkv_prep",
    )(p, p, mix, mixl, w0, w2, a0, a2, g2, k_k, k_a, e, et)


def _rwkv_scan_kernel(r_ref, lw_ref, k_ref, v_ref, an_ref, b_ref, gate_ref, rk_ref, lnw_ref,
                      lnb_ref, o_ref, h_ref, *, n_chunks):
    c2 = 2 * CHUNK

    @pl.when(pl.program_id(1) == 0)
    def _():
        h_ref[...] = jnp.zeros((LANES, LANES), F32)

    lane = _iota2((CHUNK, LANES), 1)
    head0 = lane < RWKV_N
    row = _iota2((c2, c2), 0)
    col = _iota2((c2, c2), 1)
    same = (row // CHUNK) == (col // CHUNK)
    strict = same & (row > col)
    incl = same & (row >= col)
    tr = _iota2((CHUNK, CHUNK), 0)
    tc = _iota2((CHUNK, CHUNK), 1)
    tril = jnp.where(tr >= tc, 1.0, 0.0).astype(F32)
    gr = _iota2((LANES, LANES), 0)
    gc = _iota2((LANES, LANES), 1)
    head_ones = jnp.where((gr // RWKV_N) == (gc // RWKV_N), 1.0, 0.0).astype(F32)

    def stack(t):
        return jnp.concatenate([jnp.where(head0, t, 0.0), jnp.where(head0, 0.0, t)], axis=0)

    for c in range(n_chunks):
        sl = slice(c * CHUNK, (c + 1) * CHUNK)
        r, lw, k, v = r_ref[sl, :], lw_ref[sl, :], k_ref[sl, :], v_ref[sl, :]
        an, b = an_ref[sl, :], b_ref[sl, :]
        cw = _dot(tril, lw, HI)
        e_in = jnp.exp(cw)
        e_out = jnp.exp(-cw)
        last = cw[CHUNK - 1:CHUNK, :]
        e_end = jnp.exp(last - cw)
        at = stack(an * jnp.exp(cw - lw))
        rt = stack(r * e_in)
        bt = stack(b * e_out)
        kt = stack(k * e_out)
        bd = stack(b * e_end)
        kd = stack(k * e_end)
        v2 = stack(v)

        a_ab = jnp.where(strict, _dot_nt(at, bt, HI), 0.0)
        a_ak = jnp.where(strict, _dot_nt(at, kt, HI), 0.0)
        a_rb = jnp.where(incl, _dot_nt(rt, bt, HI), 0.0)
        a_rk = jnp.where(incl, _dot_nt(rt, kt, HI), 0.0)
        tinv = _unit_lower_inverse(-a_ab, CHUNK)

        h = h_ref[...]
        u = _dot(tinv, _dot(at, h, HI) + _dot(a_ak, v2, HI), HI)
        y2 = _dot(rt, h, HI) + _dot(a_rb, u, HI) + _dot(a_rk, v2, HI)
        decay_end = jnp.exp(last)
        w_col = jnp.transpose(jnp.broadcast_to(decay_end, (LANES, LANES)))
        h_ref[...] = w_col * h + _dot_tn(bd, u, HI) + _dot_tn(kd, v2, HI)

        y = y2[0:CHUNK, :] + y2[CHUNK:c2, :]
        mu = _dot(y, head_ones, HI) * (1.0 / RWKV_N)
        yc = y - mu
        var = _dot(yc * yc, head_ones, HI) * (1.0 / RWKV_N)
        gn = yc * lax.rsqrt(var + RWKV_GN_EPS) * lnw_ref[...] + lnb_ref[...]
        bonus = _dot(r * k * rk_ref[...], head_ones, HI) * v
        o_ref[sl, :] = (gn + bonus) * gate_ref[sl, :]


def _rwkv_scan(r, lw, k, v, an, b, gate, r_k, ln_w, ln_b, tb=256):
    t = r.shape[0]
    seq = pl.BlockSpec((tb, LANES), lambda h, i: (i, h))
    par = pl.BlockSpec((1, LANES), lambda h, i: (0, h))
    return pl.pallas_call(
        functools.partial(_rwkv_scan_kernel, n_chunks=tb // CHUNK),
        grid=(D // LANES, t // tb),
        in_specs=[seq] * 7 + [par] * 3,
        out_specs=seq,
        out_shape=jax.ShapeDtypeStruct((t, D), F32),
        scratch_shapes=[pltpu.VMEM((LANES, LANES), F32)],
        compiler_params=pltpu.CompilerParams(
            dimension_semantics=("arbitrary", "arbitrary"), vmem_limit_bytes=VMEM_LIMIT),
        name="rwkv_scan",
    )(r, lw, k, v, an, b, gate, r_k, ln_w, ln_b)


def _gdn_prep_kernel(p_ref, ba_ref, cw_ref, alog_ref, dtb_ref, e_ref, et_ref,
                     q_ref, k_ref, v_ref, nb_ref, buf_ref):
    tm = p_ref.shape[0]

    @pl.when(pl.program_id(0) == 0)
    def _():
        buf_ref[0:SUBLANES, :] = jnp.zeros((SUBLANES, 3 * D), F32)

    buf_ref[SUBLANES:SUBLANES + tm, :] = p_ref[...]

    def conv_silu(lo):
        acc = jnp.zeros((tm, D), F32)
        for j in range(GDN_CONV):
            off = SUBLANES - (GDN_CONV - 1) + j
            acc = acc + buf_ref[off:off + tm, lo:lo + D] * cw_ref[j:j + 1, lo:lo + D]
        return acc * jax.nn.sigmoid(acc)

    def l2n(t):
        ss = _dot(t * t, e_ref[...], HI)
        return t * _dot(lax.rsqrt(ss + 1e-6), et_ref[...], HI)

    q_ref[...] = l2n(conv_silu(0)) * (GDN_N ** -0.5)
    k_ref[...] = l2n(conv_silu(D))
    v_ref[...] = conv_silu(2 * D)

    ba = ba_ref[...]
    lane = _iota2((tm, LANES), 1)
    g = -jnp.exp(alog_ref[...]) * jax.nn.softplus(ba + dtb_ref[...])
    g = jnp.where((lane >= GDN_HEADS) & (lane < 2 * GDN_HEADS), g, 0.0)
    tr = _iota2((tm, tm), 0)
    tc = _iota2((tm, tm), 1)
    same = (tr // CHUNK) == (tc // CHUNK)
    cum = jnp.where(same & (tr >= tc), 1.0, 0.0).astype(F32)
    tot = jnp.where(same, 1.0, 0.0).astype(F32)
    gcum = _dot(cum, g, HI)
    gtot = pltpu.roll(_dot(tot, g, HI), GDN_HEADS, 1)
    nb_ref[...] = jnp.where(lane < GDN_HEADS, jax.nn.sigmoid(ba), gcum + gtot)

    buf_ref[0:SUBLANES, :] = buf_ref[tm:tm + SUBLANES, :]


def _gdn_prep(p, conv_w, a_log, dt_bias, e, et, tm=128):
    t = p.shape[0]
    full = lambda a: pl.BlockSpec(a.shape, lambda i: (0, 0))
    big = jax.ShapeDtypeStruct((t, D), F32)
    return pl.pallas_call(
        _gdn_prep_kernel,
        grid=(t // tm,),
        in_specs=[
            pl.BlockSpec((tm, 3 * D), lambda i: (i, C_QKV // (3 * D))),
            pl.BlockSpec((tm, LANES), lambda i: (i, C_BA // LANES)),
            full(conv_w), full(a_log), full(dt_bias), full(e), full(et),
        ],
        out_specs=[pl.BlockSpec((tm, D), lambda i: (i, 0))] * 3
        + [pl.BlockSpec((tm, LANES), lambda i: (i, 0))],
        out_shape=[big] * 3 + [jax.ShapeDtypeStruct((t, LANES), F32)],
        scratch_shapes=[pltpu.VMEM((tm + SUBLANES, 3 * D), F32)],
        compiler_params=pltpu.CompilerParams(
            dimension_semantics=("arbitrary",), vmem_limit_bytes=VMEM_LIMIT),
        name="gdn_prep",
    )(p, p, conv_w, a_log, dt_bias, e, et)


def _gdn_scan_kernel(q_ref, k_ref, v_ref, nb_ref, z_ref, nw_ref, o_ref, s_ref, *, n_chunks):
    head = pl.program_id(0)

    @pl.when(pl.program_id(1) == 0)
    def _():
        s_ref[...] = jnp.zeros((GDN_N, GDN_N), F32)

    sr = _iota2((LANES, LANES), 0)
    sel_beta = jnp.where(sr == head, 1.0, 0.0).astype(F32)
    sel_gc = jnp.where(sr == head + GDN_HEADS, 1.0, 0.0).astype(F32)
    sel_gt = jnp.where(sr == head + 2 * GDN_HEADS, 1.0, 0.0).astype(F32)
    pick_gc = jnp.where(_iota2((CHUNK, LANES), 1) == head + GDN_HEADS, 1.0, 0.0).astype(F32)
    tr = _iota2((CHUNK, CHUNK), 0)
    tc = _iota2((CHUNK, CHUNK), 1)
    causal = tr >= tc
    strict = tr > tc

    for c in range(n_chunks):
        sl = slice(c * CHUNK, (c + 1) * CHUNK)
        q, k, v, nb = q_ref[sl, :], k_ref[sl, :], v_ref[sl, :], nb_ref[sl, :]
        beta = _dot(nb, sel_beta, HI)
        gcb = _dot(nb, sel_gc, HI)
        gtb = _dot(nb, sel_gt, HI)
        g_row = _dot_nt(pick_gc, nb, HI)
        dlog = gcb[:, 0:CHUNK] - g_row
        decay = jnp.where(causal, jnp.exp(jnp.minimum(dlog, 0.0)), 0.0)
        kb = k * beta
        vb = v * beta
        lower = jnp.where(strict, _dot_nt(kb, k, HI) * decay, 0.0)
        tinv = _unit_lower_inverse(lower, CHUNK)
        e_gc = jnp.exp(gcb)
        u = _dot(tinv, vb, HI)
        wk = _dot(tinv, kb * e_gc, HI)
        qk = _dot_nt(q, k, HI) * decay
        s = s_ref[...]
        v_new = u - _dot(wk, s, HI)
        o = _dot(q * e_gc, s, HI) + _dot(qk, v_new, HI)
        k_dec = k * jnp.exp(gtb - gcb)
        s_ref[...] = s * jnp.exp(gtb[0:1, :]) + _dot_tn(k_dec, v_new, HI)

        ms = jnp.mean(o * o, axis=-1, keepdims=True)
        z = z_ref[sl, :]
        o_ref[sl, :] = o * lax.rsqrt(ms + NORM_EPS) * nw_ref[...] * (z * jax.nn.sigmoid(z))


def _gdn_scan(q, k, v, nb, p, norm_w, tb=256):
    t = q.shape[0]
    seq = pl.BlockSpec((tb, LANES), lambda h, i: (i, h))
    return pl.pallas_call(
        functools.partial(_gdn_scan_kernel, n_chunks=tb // CHUNK),
        grid=(GDN_HEADS, t // tb),
        in_specs=[seq, seq, seq,
                  pl.BlockSpec((tb, LANES), lambda h, i: (i, 0)),
                  pl.BlockSpec((tb, LANES), lambda h, i: (i, C_Z // LANES + h)),
                  pl.BlockSpec((1, LANES), lambda h, i: (0, 0))],
        out_specs=seq,
        out_shape=jax.ShapeDtypeStruct((t, D), F32),
        scratch_shapes=[pltpu.VMEM((GDN_N, GDN_N), F32)],
        compiler_params=pltpu.CompilerParams(
            dimension_semantics=("arbitrary", "arbitrary"), vmem_limit_bytes=VMEM_LIMIT),
        name="gdn_scan",
    )(q, k, v, nb, p, norm_w)


def _out_proj_kernel(ya_ref, yb_ref, ga_ref, gb_ref, x_ref, w_ref, g_ref, o_ref):
    mixed = jax.nn.sigmoid(ga_ref[...]) * ya_ref[...] + jax.nn.sigmoid(gb_ref[...]) * yb_ref[...]
    y = _dot(mixed.astype(BF16), w_ref[...])
    ms = jnp.mean(y * y, axis=-1, keepdims=True)
    o_ref[...] = x_ref[...] + y * lax.rsqrt(ms + NORM_EPS) * g_ref[...]


def _out_proj(ya, yb, p, x, w_out, gain, tm=256):
    t = x.shape[0]
    blk = pl.BlockSpec((tm, D), lambda i: (i, 0))
    return pl.pallas_call(
        _out_proj_kernel,
        grid=(t // tm,),
        in_specs=[blk, blk,
                  pl.BlockSpec((tm, D), lambda i: (i, C_GA // D)),
                  pl.BlockSpec((tm, D), lambda i: (i, C_GB // D)),
                  blk,
                  pl.BlockSpec((D, D), lambda i: (0, 0)),
                  pl.BlockSpec((1, D), lambda i: (0, 0))],
        out_specs=blk,
        out_shape=jax.ShapeDtypeStruct((t, D), F32),
        compiler_params=pltpu.CompilerParams(
            dimension_semantics=("arbitrary",), vmem_limit_bytes=VMEM_LIMIT),
        name="out_proj",
    )(ya, yb, p, p, x, w_out, gain)


def _ffn_kernel(x_ref, gpre_ref, wg_ref, wu_ref, wd_ref, gpost_ref, o_ref, xn_ref, acc_ref):
    f = pl.program_id(1)

    @pl.when(f == 0)
    def _():
        x = x_ref[...]
        ms = jnp.mean(x * x, axis=-1, keepdims=True)
        xn_ref[...] = (x * lax.rsqrt(ms + NORM_EPS) * gpre_ref[...]).astype(BF16)
        acc_ref[...] = jnp.zeros(acc_ref.shape, F32)

    xn = xn_ref[...]
    gate = _dot(xn, wg_ref[...])
    up = _dot(xn, wu_ref[...])
    hidden = (gate * jax.nn.sigmoid(gate) * up).astype(BF16)
    acc_ref[...] += _dot(hidden, wd_ref[...])

    @pl.when(f == pl.num_programs(1) - 1)
    def _():
        y = acc_ref[...]
        ms = jnp.mean(y * y, axis=-1, keepdims=True)
        o_ref[...] = x_ref[...] + y * lax.rsqrt(ms + NORM_EPS) * gpost_ref[...]


def _ffn(x, g_pre, w_gate, w_up, w_down, g_post, tm=512, tf=512):
    t = x.shape[0]
    return pl.pallas_call(
        _ffn_kernel,
        grid=(t // tm, D_FF // tf),
        in_specs=[pl.BlockSpec((tm, D), lambda i, f: (i, 0)),
                  pl.BlockSpec((1, D), lambda i, f: (0, 0)),
                  pl.BlockSpec((D, tf), lambda i, f: (0, f)),
                  pl.BlockSpec((D, tf), lambda i, f: (0, f)),
                  pl.BlockSpec((tf, D), lambda i, f: (f, 0)),
                  pl.BlockSpec((1, D), lambda i, f: (0, 0))],
        out_specs=pl.BlockSpec((tm, D), lambda i, f: (i, 0)),
        out_shape=jax.ShapeDtypeStruct((t, D), F32),
        scratch_shapes=[pltpu.VMEM((tm, D), BF16), pltpu.VMEM((tm, D), F32)],
        compiler_params=pltpu.CompilerParams(
            dimension_semantics=("arbitrary", "arbitrary"), vmem_limit_bytes=VMEM_LIMIT),
        name="ffn",
    )(x, g_pre, w_gate, w_up, w_down, g_post)


def _pad_cols(w, width):
    return jnp.pad(w, ((0, 0), (0, width - w.shape[1])))


def _pad_rows(w, height):
    return jnp.pad(w, ((0, height - w.shape[0]), (0, 0)))


def _head_indicator(heads, width):
    return jnp.asarray(np.kron(np.eye(heads, dtype=np.float32), np.ones((width, 1), np.float32)))


def _layer(x, norm_mix_pre, norm_mix_post, norm_ffn_pre, norm_ffn_post, w_in, rwkv_mix, rwkv_w0,
           rwkv_w2, rwkv_a0, rwkv_a2, rwkv_g2, rwkv_k_k, rwkv_k_a, rwkv_r_k, rwkv_ln_w, rwkv_ln_b,
           gdn_conv_w, gdn_a_log, gdn_dt_bias, gdn_norm_w, w_out, ffn_w_gate, ffn_w_up, ffn_w_down):
    row = lambda v: v.reshape(1, -1).astype(F32)
    o_lw = 3 * D
    o_la = o_lw + 96
    o_lg = o_la + 96
    o_qkv = o_lg + 256
    o_z = o_qkv + 3 * D
    o_beta = o_z + D
    o_ga = o_beta + 2 * GDN_HEADS
    o_gb = o_ga + D
    cols = lambda lo, hi: w_in[:, lo:hi]
    w_all = jnp.concatenate([
        cols(0, 3 * D), cols(o_qkv, o_z), cols(o_z, o_beta), cols(o_ga, o_gb), cols(o_gb, o_gb + D),
        _pad_cols(cols(o_lw, o_la), 128), _pad_cols(cols(o_la, o_lg), 128), cols(o_lg, o_qkv),
        _pad_cols(cols(o_beta, o_ga), P_COLS - C_BA),
    ], axis=1).astype(BF16)
    mix_rkv = row(rwkv_mix[0:3 * D])
    mix_lora = jnp.concatenate([
        _pad_cols(row(rwkv_mix[o_lw:o_la]), 128), _pad_cols(row(rwkv_mix[o_la:o_lg]), 128),
        row(rwkv_mix[o_lg:o_qkv])], axis=1)
    e_r = _head_indicator(RWKV_HEADS, RWKV_N)
    e_g = _head_indicator(GDN_HEADS, GDN_N)
    lane_pad = lambda v: jnp.pad(row(v), ((0, 0), (GDN_HEADS, LANES - 2 * GDN_HEADS)))

    p = _in_proj(x, row(norm_mix_pre), w_all)
    r, lw, k, v, an, b, gate = _rwkv_prep(
        p, mix_rkv, mix_lora, row(rwkv_w0), _pad_rows(rwkv_w2, 128), row(rwkv_a0),
        _pad_rows(rwkv_a2, 128), rwkv_g2, row(rwkv_k_k), row(rwkv_k_a), e_r, e_r.T)
    y_a = _rwkv_scan(r, lw, k, v, an, b, gate, row(rwkv_r_k), row(rwkv_ln_w), row(rwkv_ln_b))
    q, kg, vg, nb = _gdn_prep(p, gdn_conv_w, lane_pad(gdn_a_log), lane_pad(gdn_dt_bias), e_g, e_g.T)
    y_b = _gdn_scan(q, kg, vg, nb, p, row(gdn_norm_w))
    x1 = _out_proj(y_a, y_b, p, x, w_out.astype(BF16), row(norm_mix_post))
    return _ffn(x1, row(norm_ffn_pre), ffn_w_gate.astype(BF16), ffn_w_up.astype(BF16),
                ffn_w_down.astype(BF16), row(norm_ffn_post))


def kernel(x, norm_mix_pre, norm_mix_post, norm_ffn_pre, norm_ffn_post, w_in, rwkv_mix, rwkv_w0,
           rwkv_w2, rwkv_a0, rwkv_a2, rwkv_g2, rwkv_k_k, rwkv_k_a, rwkv_r_k, rwkv_ln_w, rwkv_ln_b,
           gdn_conv_w, gdn_a_log, gdn_dt_bias, gdn_norm_w, w_out, ffn_w_gate, ffn_w_up, ffn_w_down):
    bsz, seq, d = x.shape
    assert (bsz, seq, d) == (1, SEQ, D)
    h = x[0]
    weights = (norm_mix_pre, norm_mix_post, norm_ffn_pre, norm_ffn_post, w_in, rwkv_mix, rwkv_w0,
               rwkv_w2, rwkv_a0, rwkv_a2, rwkv_g2, rwkv_k_k, rwkv_k_a, rwkv_r_k, rwkv_ln_w,
               rwkv_ln_b, gdn_conv_w, gdn_a_log, gdn_dt_bias, gdn_norm_w, w_out, ffn_w_gate,
               ffn_w_up, ffn_w_down)
    for layer in range(norm_mix_pre.shape[0]):
        h = _layer(h, *(w[layer] for w in weights))
    return h[None]
```

```python
import functools

import jax
import jax.numpy as jnp
import numpy as np
from jax import lax
from jax.experimental import pallas as pl
from jax.experimental.pallas import tpu as pltpu

F32 = jnp.float32
BF16 = jnp.bfloat16

D = 2048
SEQ = 8192
CHUNK = 64
NORM_EPS = 1e-6
RWKV_N = 64
RWKV_HEADS = D // RWKV_N
RWKV_GN_EPS = 64e-5
GDN_N = 128
GDN_HEADS = D // GDN_N
GDN_CONV = 4
D_FF = 5632
LANES = 128
SUBLANES = 8
VMEM_LIMIT = 56 * 1024 * 1024

C_R, C_K, C_V = 0, 2048, 4096
C_QKV = 6144
C_Z = 12288
C_GA = 14336
C_GB = 16384
C_LORA = 18432
C_BA = 18944
P_COLS = 19456
LORA_W = 512


def _dot(a, b):
    return jnp.dot(a.astype(BF16), b.astype(BF16), preferred_element_type=F32)


def _dot_nt(a, b):
    return lax.dot_general(a.astype(BF16), b.astype(BF16), (((1,), (1,)), ((), ())),
                           preferred_element_type=F32)


def _dot_tn(a, b):
    return _dot(jnp.transpose(a), b)


def _split_bf16(x, passes):
    parts = []
    for _ in range(passes - 1):
        hi = x.astype(BF16)
        parts.append(hi)
        x = x - hi.astype(F32)
    parts.append(x.astype(BF16))
    return parts


def _dot01_right(x, m01, passes):
    m = m01.astype(BF16)
    return sum(jnp.dot(p, m, preferred_element_type=F32) for p in _split_bf16(x, passes))


def _dot01_left(m01, x, passes):
    m = m01.astype(BF16)
    return sum(jnp.dot(m, p, preferred_element_type=F32) for p in _split_bf16(x, passes))


def _iota2(shape, dim):
    return lax.broadcasted_iota(jnp.int32, shape, dim)


def _unit_lower_inverses(lows, n_block):
    n = lows[0].shape[0]
    row = _iota2((n, n), 0)
    col = _iota2((n, n), 1)
    eye = jnp.where(row == col, 1.0, 0.0).astype(F32)
    xs = [eye] * len(lows)
    b = 1
    while b < n_block:
        sel = ((row // (2 * b)) == (col // (2 * b))) & ((row // b) % 2 == 1) & ((col // b) % 2 == 0)
        cs = [jnp.where(sel, low, 0.0) for low in lows]
        if b == 1:
            xs = [x - c for x, c in zip(xs, cs)]
        else:
            xc = [_dot(x, c) for x, c in zip(xs, cs)]
            xs = [x - _dot(t, x) for x, t in zip(xs, xc)]
        b *= 2
    return xs


def _in_proj_kernel(x_ref, g_ref, w_ref, o_ref, xn_ref):
    @pl.when(pl.program_id(1) == 0)
    def _():
        x = x_ref[...]
        ms = jnp.mean(x * x, axis=-1, keepdims=True)
        xn_ref[...] = (x * lax.rsqrt(ms + NORM_EPS) * g_ref[...]).astype(BF16)

    o_ref[...] = _dot(xn_ref[...], w_ref[...])


def _in_proj(x, gain, w_all, tm=1024, tn=512):
    t = x.shape[0]
    return pl.pallas_call(
        _in_proj_kernel,
        grid=(t // tm, P_COLS // tn),
        in_specs=[
            pl.BlockSpec((tm, D), lambda i, j: (i, 0)),
            pl.BlockSpec((1, D), lambda i, j: (0, 0)),
            pl.BlockSpec((D, tn), lambda i, j: (0, j)),
        ],
        out_specs=pl.BlockSpec((tm, tn), lambda i, j: (i, j)),
        out_shape=jax.ShapeDtypeStruct((t, P_COLS), F32),
        scratch_shapes=[pltpu.VMEM((tm, D), BF16)],
        compiler_params=pltpu.CompilerParams(
            dimension_semantics=("arbitrary", "arbitrary"), vmem_limit_bytes=VMEM_LIMIT),
        name="in_proj",
    )(x, gain, w_all)


def _rwkv_prep_kernel(p_ref, pl_ref, mix_ref, mixl_ref, w0_ref, w2_ref, a0_ref, a2_ref, g2_ref,
                      kk_ref, ka_ref, e_ref, et_ref,
                      r_ref, lw_ref, k_ref, v_ref, an_ref, b_ref, gate_ref,
                      buf_ref, bufl_ref):
    tm = p_ref.shape[0]

    @pl.when(pl.program_id(0) == 0)
    def _():
        buf_ref[0:SUBLANES, :] = jnp.zeros((SUBLANES, 3 * D), F32)
        bufl_ref[0:SUBLANES, :] = jnp.zeros((SUBLANES, LORA_W), F32)

    buf_ref[SUBLANES:SUBLANES + tm, :] = p_ref[...]
    bufl_ref[SUBLANES:SUBLANES + tm, :] = pl_ref[...]

    def shifted(ref, lo, hi, mix):
        cur = ref[SUBLANES:SUBLANES + tm, lo:hi]
        prev = ref[SUBLANES - 1:SUBLANES - 1 + tm, lo:hi]
        return cur + (prev - cur) * mix

    r = shifted(buf_ref, C_R, C_R + D, mix_ref[:, C_R:C_R + D])
    k = shifted(buf_ref, C_K, C_K + D, mix_ref[:, C_K:C_K + D])
    v = shifted(buf_ref, C_V, C_V + D, mix_ref[:, C_V:C_V + D])
    lora = shifted(bufl_ref, 0, LORA_W, mixl_ref[...])
    lw = lora[:, 0:128]
    la = lora[:, 128:256]
    lg = lora[:, 256:512]

    w_log = -jax.nn.softplus(-(w0_ref[...] + _dot(jnp.tanh(lw), w2_ref[...]))) - 0.5
    a = jax.nn.sigmoid(a0_ref[...] + _dot(la, a2_ref[...]))
    gate = _dot(jax.nn.sigmoid(lg), g2_ref[...])

    kk = k * kk_ref[...]
    ss = _dot01_right(kk * kk, e_ref[...], 2)
    kk = kk * _dot01_right(lax.rsqrt(ss + 1e-6), et_ref[...], 3)

    r_ref[...] = r
    lw_ref[...] = -jnp.exp(w_log)
    k_ref[...] = k * (1.0 + (a - 1.0) * ka_ref[...])
    v_ref[...] = v
    an_ref[...] = -kk
    b_ref[...] = kk * a
    gate_ref[...] = gate

    buf_ref[0:SUBLANES, :] = buf_ref[tm:tm + SUBLANES, :]
    bufl_ref[0:SUBLANES, :] = bufl_ref[tm:tm + SUBLANES, :]


def _rwkv_prep(p, mix, mixl, w0, w2, a0, a2, g2, k_k, k_a, e, et, tm=128):
    t = p.shape[0]
    row = lambda w: pl.BlockSpec((1, w), lambda i: (0, 0))
    full = lambda a: pl.BlockSpec(a.shape, lambda i: (0, 0))
    out = jax.ShapeDtypeStruct((t, D), F32)
    return pl.pallas_call(
        _rwkv_prep_kernel,
        grid=(t // tm,),
        in_specs=[
            pl.BlockSpec((tm, 3 * D), lambda i: (i, 0)),
            pl.BlockSpec((tm, LORA_W), lambda i: (i, C_LORA // LORA_W)),
            row(3 * D), row(LORA_W), row(D), full(w2), row(D), full(a2), full(g2),
            row(D), row(D), full(e), full(et),
        ],
        out_specs=[pl.BlockSpec((tm, D), lambda i: (i, 0))] * 7,
        out_shape=[out] * 7,
        scratch_shapes=[pltpu.VMEM((tm + SUBLANES, 3 * D), F32),
                        pltpu.VMEM((tm + SUBLANES, LORA_W), F32)],
        compiler_params=pltpu.CompilerParams(
            dimension_semantics=("arbitrary",), vmem_limit_bytes=VMEM_LIMIT),
        name="rwkv_prep",
    )(p, p, mix, mixl, w0, w2, a0, a2, g2, k_k, k_a, e, et)


def _rwkv_scan_kernel(r_ref, lw_ref, k_ref, v_ref, an_ref, b_ref, gate_ref, rk_ref, lnw_ref,
                      lnb_ref, o_ref, h_ref, *, n_chunks):
    c2 = 2 * CHUNK

    @pl.when(pl.program_id(1) == 0)
    def _():
        h_ref[...] = jnp.zeros((LANES, LANES), F32)

    lane = _iota2((CHUNK, LANES), 1)
    head0 = lane < RWKV_N
    row = _iota2((c2, c2), 0)
    col = _iota2((c2, c2), 1)
    same = (row // CHUNK) == (col // CHUNK)
    strict = same & (row > col)
    incl = same & (row >= col)
    tb = n_chunks * CHUNK
    tr = _iota2((tb, tb), 0)
    tc = _iota2((tb, tb), 1)
    tril = jnp.where(((tr // CHUNK) == (tc // CHUNK)) & (tr >= tc), 1.0, 0.0)
    gr = _iota2((LANES, LANES), 0)
    gc = _iota2((LANES, LANES), 1)
    head_ones = jnp.where((gr // RWKV_N) == (gc // RWKV_N), 1.0, 0.0)

    def stack(t):
        return jnp.concatenate([jnp.where(head0, t, 0.0), jnp.where(head0, 0.0, t)], axis=0)

    chunks = range(n_chunks)
    rows = [slice(c * CHUNK, (c + 1) * CHUNK) for c in chunks]
    lw_all = lw_ref[...]
    cw_all = _dot01_left(tril, lw_all, 3)

    at, rt, bt, kt, bd, kd, v2, w_col = [], [], [], [], [], [], [], []
    for sl in rows:
        cw = cw_all[sl, :]
        last = cw[CHUNK - 1:CHUNK, :]
        e_out = jnp.exp(-cw)
        e_end = jnp.exp(last - cw)
        k, b = k_ref[sl, :], b_ref[sl, :]
        at.append(stack(an_ref[sl, :] * jnp.exp(cw - lw_all[sl, :])))
        rt.append(stack(r_ref[sl, :] * jnp.exp(cw)))
        bt.append(stack(b * e_out))
        kt.append(stack(k * e_out))
        bd.append(stack(b * e_end))
        kd.append(stack(k * e_end))
        v2.append(stack(v_ref[sl, :]))
        w_col.append(jnp.transpose(jnp.broadcast_to(jnp.exp(last), (LANES, LANES))))
    sc = [_dot_nt(jnp.concatenate([at[c], rt[c]], axis=0), jnp.concatenate([bt[c], kt[c]], axis=0))
          for c in chunks]
    a_ak = [jnp.where(strict, s[0:c2, c2:2 * c2], 0.0) for s in sc]
    a_rb = [jnp.where(incl, s[c2:2 * c2, 0:c2], 0.0) for s in sc]
    a_rk = [jnp.where(incl, s[c2:2 * c2, c2:2 * c2], 0.0) for s in sc]
    tinv = _unit_lower_inverses([jnp.where(strict, -s[0:c2, 0:c2], 0.0) for s in sc], CHUNK)
    akv = [_dot(a_ak[c], v2[c]) for c in chunks]
    pu = [_dot(tinv[c], jnp.concatenate([at[c], akv[c]], axis=1)) for c in chunks]
    mn = [_dot_tn(bd[c], pu[c]) for c in chunks]
    qy = [_dot(a_rb[c], pu[c]) for c in chunks]
    n_add = [mn[c][:, LANES:] + _dot_tn(kd[c], v2[c]) for c in chunks]
    y0 = [qy[c][:, LANES:] + _dot(a_rk[c], v2[c]) for c in chunks]

    h = h_ref[...]
    ys = []
    for c in chunks:
        y2 = _dot(rt[c] + qy[c][:, 0:LANES], h) + y0[c]
        h = w_col[c] * h + _dot(mn[c][:, 0:LANES], h) + n_add[c]
        ys.append(y2[0:CHUNK, :] + y2[CHUNK:c2, :])
    h_ref[...] = h

    y = jnp.concatenate(ys, axis=0)
    mu = _dot(y, head_ones) * (1.0 / RWKV_N)
    yc = y - mu
    var = _dot(yc * yc, head_ones) * (1.0 / RWKV_N)
    gn = yc * lax.rsqrt(var + RWKV_GN_EPS) * lnw_ref[...] + lnb_ref[...]
    bonus = _dot01_right(r_ref[...] * k_ref[...] * rk_ref[...], head_ones, 2) * v_ref[...]
    o_ref[...] = (gn + bonus) * gate_ref[...]


def _rwkv_scan(r, lw, k, v, an, b, gate, r_k, ln_w, ln_b, tb=512):
    t = r.shape[0]
    seq = pl.BlockSpec((tb, LANES), lambda h, i: (i, h))
    par = pl.BlockSpec((1, LANES), lambda h, i: (0, h))
    return pl.pallas_call(
        functools.partial(_rwkv_scan_kernel, n_chunks=tb // CHUNK),
        grid=(D // LANES, t // tb),
        in_specs=[seq] * 7 + [par] * 3,
        out_specs=seq,
        out_shape=jax.ShapeDtypeStruct((t, D), F32),
        scratch_shapes=[pltpu.VMEM((LANES, LANES), F32)],
        compiler_params=pltpu.CompilerParams(
            dimension_semantics=("arbitrary", "arbitrary"), vmem_limit_bytes=VMEM_LIMIT),
        name="rwkv_scan",
    )(r, lw, k, v, an, b, gate, r_k, ln_w, ln_b)


def _gdn_prep_kernel(p_ref, ba_ref, cw_ref, alog_ref, dtb_ref, e_ref, et_ref, eb_ref, eg_ref,
                     q_ref, k_ref, kb_ref, vb_ref, gc_ref, buf_ref):
    tm = p_ref.shape[0]

    @pl.when(pl.program_id(0) == 0)
    def _():
        buf_ref[0:SUBLANES, :] = jnp.zeros((SUBLANES, 3 * D), F32)

    buf_ref[SUBLANES:SUBLANES + tm, :] = p_ref[...]

    def conv_silu(lo):
        acc = jnp.zeros((tm, D), F32)
        for j in range(GDN_CONV):
            off = SUBLANES - (GDN_CONV - 1) + j
            acc = acc + buf_ref[off:off + tm, lo:lo + D] * cw_ref[j:j + 1, lo:lo + D]
        return acc * jax.nn.sigmoid(acc)

    def l2n(t):
        ss = _dot01_right(t * t, e_ref[...], 2)
        return t * _dot01_right(lax.rsqrt(ss + 1e-6), et_ref[...], 3)

    ba = ba_ref[...]
    g = -jnp.exp(alog_ref[...]) * jax.nn.softplus(ba + dtb_ref[...])
    tr = _iota2((tm, tm), 0)
    tc = _iota2((tm, tm), 1)
    cum = jnp.where(((tr // CHUNK) == (tc // CHUNK)) & (tr >= tc), 1.0, 0.0)
    gcum = _dot01_left(cum, g, 3)
    beta = _dot01_right(jax.nn.sigmoid(ba), eb_ref[...], 3)
    gc_ref[...] = _dot01_right(gcum, eg_ref[...], 3)

    q_ref[...] = l2n(conv_silu(0)) * (GDN_N ** -0.5)
    k = l2n(conv_silu(D))
    k_ref[...] = k
    kb_ref[...] = k * beta
    vb_ref[...] = conv_silu(2 * D) * beta

    buf_ref[0:SUBLANES, :] = buf_ref[tm:tm + SUBLANES, :]


def _gdn_prep(p, conv_w, a_log, dt_bias, e, et, eb, eg, tm=128):
    t = p.shape[0]
    full = lambda a: pl.BlockSpec(a.shape, lambda i: (0, 0))
    big = jax.ShapeDtypeStruct((t, D), F32)
    return pl.pallas_call(
        _gdn_prep_kernel,
        grid=(t // tm,),
        in_specs=[
            pl.BlockSpec((tm, 3 * D), lambda i: (i, C_QKV // (3 * D))),
            pl.BlockSpec((tm, LANES), lambda i: (i, C_BA // LANES)),
            full(conv_w), full(a_log), full(dt_bias), full(e), full(et), full(eb), full(eg),
        ],
        out_specs=[pl.BlockSpec((tm, D), lambda i: (i, 0))] * 5,
        out_shape=[big] * 5,
        scratch_shapes=[pltpu.VMEM((tm + SUBLANES, 3 * D), F32)],
        compiler_params=pltpu.CompilerParams(
            dimension_semantics=("arbitrary",), vmem_limit_bytes=VMEM_LIMIT),
        name="gdn_prep",
    )(p, p, conv_w, a_log, dt_bias, e, et, eb, eg)


def _gdn_scan_kernel(q_ref, k_ref, kb_ref, vb_ref, gc_ref, z_ref, nw_ref, o_ref, s_ref, *,
                     n_chunks):
    @pl.when(pl.program_id(1) == 0)
    def _():
        s_ref[...] = jnp.zeros((GDN_N, GDN_N), F32)

    tr = _iota2((CHUNK, CHUNK), 0)
    tc = _iota2((CHUNK, CHUNK), 1)
    causal = tr >= tc
    strict = tr > tc

    chunks = range(n_chunks)
    rows = [slice(c * CHUNK, (c + 1) * CHUNK) for c in chunks]

    q = [q_ref[sl, :] for sl in rows]
    k = [k_ref[sl, :] for sl in rows]
    kb = [kb_ref[sl, :] for sl in rows]
    gcb = [gc_ref[sl, :] for sl in rows]
    decay = []
    for c in chunks:
        g_row = jnp.transpose(gcb[c])[0:CHUNK, :]
        dlog = gcb[c][:, 0:CHUNK] - g_row
        decay.append(jnp.where(causal, jnp.exp(jnp.minimum(dlog, 0.0)), 0.0))
    sc = [_dot_nt(jnp.concatenate([kb[c], q[c]], axis=0), k[c]) for c in chunks]
    qk = [sc[c][CHUNK:2 * CHUNK, :] * decay[c] for c in chunks]
    tinv = _unit_lower_inverses(
        [jnp.where(strict, sc[c][0:CHUNK, :] * decay[c], 0.0) for c in chunks], CHUNK)
    e_gc = [jnp.exp(g) for g in gcb]
    g_last = [g[CHUNK - 1:CHUNK, :] for g in gcb]
    uw = [_dot(tinv[c], jnp.concatenate([vb_ref[rows[c], :], kb[c] * e_gc[c]], axis=1))
          for c in chunks]
    mn = [_dot_tn(k[c] * jnp.exp(g_last[c] - gcb[c]), uw[c]) for c in chunks]
    qq = [_dot(qk[c], uw[c]) for c in chunks]

    s = s_ref[...]
    os = []
    for c in chunks:
        os.append(_dot(q[c] * e_gc[c] - qq[c][:, LANES:], s) + qq[c][:, 0:LANES])
        s = jnp.exp(g_last[c]) * s - _dot(mn[c][:, LANES:], s) + mn[c][:, 0:LANES]
    s_ref[...] = s

    o = jnp.concatenate(os, axis=0)
    ms = jnp.mean(o * o, axis=-1, keepdims=True)
    z = z_ref[...]
    o_ref[...] = o * lax.rsqrt(ms + NORM_EPS) * nw_ref[...] * (z * jax.nn.sigmoid(z))


def _gdn_scan(q, k, kb, vb, gc, p, norm_w, tb=1024):
    t = q.shape[0]
    seq = pl.BlockSpec((tb, LANES), lambda h, i: (i, h))
    return pl.pallas_call(
        functools.partial(_gdn_scan_kernel, n_chunks=tb // CHUNK),
        grid=(GDN_HEADS, t // tb),
        in_specs=[seq] * 5 + [
            pl.BlockSpec((tb, LANES), lambda h, i: (i, C_Z // LANES + h)),
            pl.BlockSpec((1, LANES), lambda h, i: (0, 0))],
        out_specs=seq,
        out_shape=jax.ShapeDtypeStruct((t, D), F32),
        scratch_shapes=[pltpu.VMEM((GDN_N, GDN_N), F32)],
        compiler_params=pltpu.CompilerParams(
            dimension_semantics=("arbitrary", "arbitrary"), vmem_limit_bytes=VMEM_LIMIT),
        name="gdn_scan",
    )(q, k, kb, vb, gc, p, norm_w)


def _out_proj_kernel(ya_ref, yb_ref, ga_ref, gb_ref, x_ref, w_ref, g_ref, o_ref):
    mixed = jax.nn.sigmoid(ga_ref[...]) * ya_ref[...] + jax.nn.sigmoid(gb_ref[...]) * yb_ref[...]
    y = _dot(mixed.astype(BF16), w_ref[...])
    ms = jnp.mean(y * y, axis=-1, keepdims=True)
    o_ref[...] = x_ref[...] + y * lax.rsqrt(ms + NORM_EPS) * g_ref[...]


def _out_proj(ya, yb, p, x, w_out, gain, tm=256):
    t = x.shape[0]
    blk = pl.BlockSpec((tm, D), lambda i: (i, 0))
    return pl.pallas_call(
        _out_proj_kernel,
        grid=(t // tm,),
        in_specs=[blk, blk,
                  pl.BlockSpec((tm, D), lambda i: (i, C_GA // D)),
                  pl.BlockSpec((tm, D), lambda i: (i, C_GB // D)),
                  blk,
                  pl.BlockSpec((D, D), lambda i: (0, 0)),
                  pl.BlockSpec((1, D), lambda i: (0, 0))],
        out_specs=blk,
        out_shape=jax.ShapeDtypeStruct((t, D), F32),
        compiler_params=pltpu.CompilerParams(
            dimension_semantics=("arbitrary",), vmem_limit_bytes=VMEM_LIMIT),
        name="out_proj",
    )(ya, yb, p, p, x, w_out, gain)


def _ffn_kernel(x_ref, gpre_ref, wg_ref, wu_ref, wd_ref, gpost_ref, o_ref, xn_ref, acc_ref):
    f = pl.program_id(1)

    @pl.when(f == 0)
    def _():
        x = x_ref[...]
        ms = jnp.mean(x * x, axis=-1, keepdims=True)
        xn_ref[...] = (x * lax.rsqrt(ms + NORM_EPS) * gpre_ref[...]).astype(BF16)
        acc_ref[...] = jnp.zeros(acc_ref.shape, F32)

    xn = xn_ref[...]
    gate = _dot(xn, wg_ref[...])
    up = _dot(xn, wu_ref[...])
    hidden = (gate * jax.nn.sigmoid(gate) * up).astype(BF16)
    acc_ref[...] += _dot(hidden, wd_ref[...])

    @pl.when(f == pl.num_programs(1) - 1)
    def _():
        y = acc_ref[...]
        ms = jnp.mean(y * y, axis=-1, keepdims=True)
        o_ref[...] = x_ref[...] + y * lax.rsqrt(ms + NORM_EPS) * gpost_ref[...]


def _ffn(x, g_pre, w_gate, w_up, w_down, g_post, tm=512, tf=512):
    t = x.shape[0]
    return pl.pallas_call(
        _ffn_kernel,
        grid=(t // tm, D_FF // tf),
        in_specs=[pl.BlockSpec((tm, D), lambda i, f: (i, 0)),
                  pl.BlockSpec((1, D), lambda i, f: (0, 0)),
                  pl.BlockSpec((D, tf), lambda i, f: (0, f)),
                  pl.BlockSpec((D, tf), lambda i, f: (0, f)),
                  pl.BlockSpec((tf, D), lambda i, f: (f, 0)),
                  pl.BlockSpec((1, D), lambda i, f: (0, 0))],
        out_specs=pl.BlockSpec((tm, D), lambda i, f: (i, 0)),
        out_shape=jax.ShapeDtypeStruct((t, D), F32),
        scratch_shapes=[pltpu.VMEM((tm, D), BF16), pltpu.VMEM((tm, D), F32)],
        compiler_params=pltpu.CompilerParams(
            dimension_semantics=("arbitrary", "arbitrary"), vmem_limit_bytes=VMEM_LIMIT),
        name="ffn",
    )(x, g_pre, w_gate, w_up, w_down, g_post)


def _pad_cols(w, width):
    return jnp.pad(w, ((0, 0), (0, width - w.shape[1])))


def _pad_rows(w, height):
    return jnp.pad(w, ((0, height - w.shape[0]), (0, 0)))


def _head_indicator(heads, width):
    return jnp.asarray(np.kron(np.eye(heads, dtype=np.float32), np.ones((width, 1), np.float32)))


def _layer(x, norm_mix_pre, norm_mix_post, norm_ffn_pre, norm_ffn_post, w_in, rwkv_mix, rwkv_w0,
           rwkv_w2, rwkv_a0, rwkv_a2, rwkv_g2, rwkv_k_k, rwkv_k_a, rwkv_r_k, rwkv_ln_w, rwkv_ln_b,
           gdn_conv_w, gdn_a_log, gdn_dt_bias, gdn_norm_w, w_out, ffn_w_gate, ffn_w_up, ffn_w_down):
    row = lambda v: v.reshape(1, -1).astype(F32)
    o_lw = 3 * D
    o_la = o_lw + 96
    o_lg = o_la + 96
    o_qkv = o_lg + 256
    o_z = o_qkv + 3 * D
    o_beta = o_z + D
    o_ga = o_beta + 2 * GDN_HEADS
    o_gb = o_ga + D
    cols = lambda lo, hi: w_in[:, lo:hi]
    w_all = jnp.concatenate([
        cols(0, 3 * D), cols(o_qkv, o_z), cols(o_z, o_beta), cols(o_ga, o_gb), cols(o_gb, o_gb + D),
        _pad_cols(cols(o_lw, o_la), 128), _pad_cols(cols(o_la, o_lg), 128), cols(o_lg, o_qkv),
        _pad_cols(cols(o_beta, o_ga), P_COLS - C_BA),
    ], axis=1).astype(BF16)
    mix_rkv = row(rwkv_mix[0:3 * D])
    mix_lora = jnp.concatenate([
        _pad_cols(row(rwkv_mix[o_lw:o_la]), 128), _pad_cols(row(rwkv_mix[o_la:o_lg]), 128),
        row(rwkv_mix[o_lg:o_qkv])], axis=1)
    e_r = _head_indicator(RWKV_HEADS, RWKV_N).astype(BF16)
    e_g = _head_indicator(GDN_HEADS, GDN_N).astype(BF16)
    e_beta = _pad_rows(e_g.T, LANES)
    e_gdec = jnp.pad(e_g.T, ((GDN_HEADS, LANES - 2 * GDN_HEADS), (0, 0)))
    lane_pad = lambda v: jnp.pad(row(v), ((0, 0), (GDN_HEADS, LANES - 2 * GDN_HEADS)))

    p = _in_proj(x, row(norm_mix_pre), w_all)
    r, lw, k, v, an, b, gate = _rwkv_prep(
        p, mix_rkv, mix_lora, row(rwkv_w0), _pad_rows(rwkv_w2, 128).astype(BF16), row(rwkv_a0),
        _pad_rows(rwkv_a2, 128).astype(BF16), rwkv_g2.astype(BF16), row(rwkv_k_k), row(rwkv_k_a),
        e_r, e_r.T)
    y_a = _rwkv_scan(r, lw, k, v, an, b, gate, row(rwkv_r_k), row(rwkv_ln_w), row(rwkv_ln_b))
    q, kg, kb, vb, gc = _gdn_prep(p, gdn_conv_w, lane_pad(gdn_a_log), lane_pad(gdn_dt_bias),
                                  e_g, e_g.T, e_beta, e_gdec)
    y_b = _gdn_scan(q, kg, kb, vb, gc, p, row(gdn_norm_w))
    x1 = _out_proj(y_a, y_b, p, x, w_out.astype(BF16), row(norm_mix_post))
    return _ffn(x1, row(norm_ffn_pre), ffn_w_gate.astype(BF16), ffn_w_up.astype(BF16),
                ffn_w_down.astype(BF16), row(norm_ffn_post))


def kernel(x, norm_mix_pre, norm_mix_post, norm_ffn_pre, norm_ffn_post, w_in, rwkv_mix, rwkv_w0,
           rwkv_w2, rwkv_a0, rwkv_a2, rwkv_g2, rwkv_k_k, rwkv_k_a, rwkv_r_k, rwkv_ln_w, rwkv_ln_b,
           gdn_conv_w, gdn_a_log, gdn_dt_bias, gdn_norm_w, w_out, ffn_w_gate, ffn_w_up, ffn_w_down):
    bsz, seq, d = x.shape
    assert (bsz, seq, d) == (1, SEQ, D)
    h = x[0]
    weights = (norm_mix_pre, norm_mix_post, norm_ffn_pre, norm_ffn_post, w_in, rwkv_mix, rwkv_w0,
               rwkv_w2, rwkv_a0, rwkv_a2, rwkv_g2, rwkv_k_k, rwkv_k_a, rwkv_r_k, rwkv_ln_w,
               rwkv_ln_b, gdn_conv_w, gdn_a_log, gdn_dt_bias, gdn_norm_w, w_out, ffn_w_gate,
               ffn_w_up, ffn_w_down)
    for layer in range(norm_mix_pre.shape[0]):
        h = _layer(h, *(w[layer] for w in weights))
    return h[None]
```

```python
import functools

import jax
import jax.numpy as jnp
from jax import lax
from jax.experimental import pallas as pl
from jax.experimental.pallas import tpu as pltpu

F32 = jnp.float32
BF16 = jnp.bfloat16

D = 2048
SEQ = 8192
CHUNK = 64
NORM_EPS = 1e-6
RWKV_N = 64
RWKV_GN_EPS = 64e-5
GDN_N = 128
GDN_HEADS = D // GDN_N
GDN_CONV = 4
D_FF = 5632
LANES = 128
SUBLANES = 8
VMEM_LIMIT = 56 * 1024 * 1024

C_R, C_K, C_V = 0, 2048, 4096
C_QKV = 6144
C_Z = 12288
C_GA = 14336
C_GB = 16384
C_LORA = 18432
C_BA = 18944
P_COLS = 19456
LORA_W = 512
(ROW_MIX_R, ROW_MIX_K, ROW_MIX_V, ROW_W0, ROW_A0, ROW_KK, ROW_KA, ROW_RK, ROW_LNW, ROW_LNB) = range(10)
RWKV_ROWS = 16


def _dot(a, b):
    return jnp.dot(a.astype(BF16), b.astype(BF16), preferred_element_type=F32)


def _dot_nt(a, b):
    return lax.dot_general(a.astype(BF16), b.astype(BF16), (((1,), (1,)), ((), ())),
                           preferred_element_type=F32)


def _dot_tn(a, b):
    return _dot(jnp.transpose(a), b)


def _split_bf16(x, passes):
    parts = []
    for _ in range(passes - 1):
        hi = x.astype(BF16)
        parts.append(hi)
        x = x - hi.astype(F32)
    parts.append(x.astype(BF16))
    return parts


def _dot01_right(x, m01, passes):
    m = m01.astype(BF16)
    return sum(jnp.dot(p, m, preferred_element_type=F32) for p in _split_bf16(x, passes))


def _dot01_left(m01, x, passes):
    m = m01.astype(BF16)
    return sum(jnp.dot(m, p, preferred_element_type=F32) for p in _split_bf16(x, passes))


def _iota2(shape, dim):
    return lax.broadcasted_iota(jnp.int32, shape, dim)


def _chunk_cumsum(x):
    n = 2 * CHUNK
    row = _iota2((n, n), 0)
    col = _iota2((n, n), 1)
    tril = jnp.where(((row // CHUNK) == (col // CHUNK)) & (row >= col), 1.0, 0.0)
    groups = [_dot01_left(tril, x[g:g + n, :], 3) for g in range(0, x.shape[0], n)]
    return jnp.concatenate(groups, axis=0)


def _unit_lower_inverses(lows, n_block):
    n = lows[0].shape[0]
    row = _iota2((n, n), 0)
    col = _iota2((n, n), 1)
    eye = jnp.where(row == col, 1.0, 0.0).astype(F32)
    xs = [eye] * len(lows)
    b = 1
    while b < n_block:
        sel = ((row // (2 * b)) == (col // (2 * b))) & ((row // b) % 2 == 1) & ((col // b) % 2 == 0)
        cs = [jnp.where(sel, low, 0.0) for low in lows]
        if b == 1:
            xs = [x - c for x, c in zip(xs, cs)]
        else:
            xc = [_dot(x, c) for x, c in zip(xs, cs)]
            xs = [x - _dot(t, x) for x, t in zip(xs, xc)]
        b *= 2
    return xs


def _in_proj_kernel(x_ref, g_ref, w_ref, o_ref, xn_ref):
    @pl.when(pl.program_id(1) == 0)
    def _():
        x = x_ref[...]
        ms = jnp.mean(x * x, axis=-1, keepdims=True)
        xn_ref[...] = (x * lax.rsqrt(ms + NORM_EPS) * g_ref[...]).astype(BF16)

    o_ref[...] = _dot(xn_ref[...], w_ref[...])


def _in_proj(x, gain, w_all, tm=1024, tn=1024):
    t = x.shape[0]
    return pl.pallas_call(
        _in_proj_kernel,
        grid=(t // tm, P_COLS // tn),
        in_specs=[
            pl.BlockSpec((tm, D), lambda i, j: (i, 0)),
            pl.BlockSpec((1, D), lambda i, j: (0, 0)),
            pl.BlockSpec((D, tn), lambda i, j: (0, j)),
        ],
        out_specs=pl.BlockSpec((tm, tn), lambda i, j: (i, j)),
        out_shape=jax.ShapeDtypeStruct((t, P_COLS), F32),
        scratch_shapes=[pltpu.VMEM((tm, D), BF16)],
        compiler_params=pltpu.CompilerParams(
            dimension_semantics=("arbitrary", "arbitrary"), vmem_limit_bytes=VMEM_LIMIT),
        name="in_proj",
    )(x, gain, w_all)


def _rwkv_mix_kernel(pr_ref, pk_ref, pv_ref, plora_ref, rows_ref, mixl_ref, w2_ref, a2_ref, g2_ref,
                     o_ref, buf_ref, h_ref, *, n_chunks, n_pairs):
    c2 = 2 * CHUNK
    tb = n_chunks * CHUNK
    width = n_pairs * LANES
    lo_k, lo_v, lo_l = width, 2 * width, 3 * width

    @pl.when(pl.program_id(1) == 0)
    def _():
        buf_ref[0:SUBLANES, :] = jnp.zeros((SUBLANES, lo_l + LORA_W), F32)
        h_ref[...] = jnp.zeros((n_pairs, LANES, LANES), F32)

    buf_ref[SUBLANES:SUBLANES + tb, 0:lo_k] = pr_ref[...]
    buf_ref[SUBLANES:SUBLANES + tb, lo_k:lo_v] = pk_ref[...]
    buf_ref[SUBLANES:SUBLANES + tb, lo_v:lo_l] = pv_ref[...]
    buf_ref[SUBLANES:SUBLANES + tb, lo_l:lo_l + LORA_W] = plora_ref[...]
    par = rows_ref[...]
    prow = lambda j: par[j:j + 1, :]
    mix = jnp.concatenate([prow(ROW_MIX_R), prow(ROW_MIX_K), prow(ROW_MIX_V), mixl_ref[...]], axis=1)
    cur = buf_ref[SUBLANES:SUBLANES + tb, :]
    xs = cur + (buf_ref[SUBLANES - 1:SUBLANES - 1 + tb, :] - cur) * mix
    buf_ref[0:SUBLANES, :] = buf_ref[tb:tb + SUBLANES, :]
    r, k, v = xs[:, 0:lo_k], xs[:, lo_k:lo_v], xs[:, lo_v:lo_l]
    lw_in, la_in, lg_in = xs[:, lo_l:lo_l + 128], xs[:, lo_l + 128:lo_l + 256], xs[:, lo_l + 256:]

    gr = _iota2((width, width), 0)
    gc = _iota2((width, width), 1)
    head_ones = jnp.where((gr // RWKV_N) == (gc // RWKV_N), 1.0, 0.0)

    w_log = -jax.nn.softplus(-(prow(ROW_W0) + _dot(jnp.tanh(lw_in), w2_ref[...]))) - 0.5
    lw_all = -jnp.exp(w_log)
    a = jax.nn.sigmoid(prow(ROW_A0) + _dot(la_in, a2_ref[...]))
    gate = _dot(jax.nn.sigmoid(lg_in), g2_ref[...])
    kk = k * prow(ROW_KK)
    kk = kk * lax.rsqrt(_dot01_right(kk * kk, head_ones, 2) + 1e-6)
    k = k * (1.0 + (a - 1.0) * prow(ROW_KA))
    an_all = -kk
    b_all = kk * a
    cw_all = _chunk_cumsum(lw_all)

    lane = _iota2((CHUNK, LANES), 1)
    head0 = lane < RWKV_N
    row = _iota2((c2, c2), 0)
    col = _iota2((c2, c2), 1)
    same = (row // CHUNK) == (col // CHUNK)
    strict = same & (row > col)
    incl = same & (row >= col)

    def stack(t):
        return jnp.concatenate([jnp.where(head0, t, 0.0), jnp.where(head0, 0.0, t)], axis=0)

    units = [(c, j) for c in range(n_chunks) for j in range(n_pairs)]
    us = range(len(units))
    cut = lambda t, c, j: t[c * CHUNK:(c + 1) * CHUNK, j * LANES:(j + 1) * LANES]

    at, rt, bt, kt, bd, kd, v2, w_col = [], [], [], [], [], [], [], []
    for c, j in units:
        cw = cut(cw_all, c, j)
        last = cw[CHUNK - 1:CHUNK, :]
        e_out = jnp.exp(-cw)
        e_end = jnp.exp(last - cw)
        at.append(stack(cut(an_all, c, j) * jnp.exp(cw - cut(lw_all, c, j))))
        rt.append(stack(cut(r, c, j) * jnp.exp(cw)))
        bt.append(stack(cut(b_all, c, j) * e_out))
        kt.append(stack(cut(k, c, j) * e_out))
        bd.append(stack(cut(b_all, c, j) * e_end))
        kd.append(stack(cut(k, c, j) * e_end))
        v2.append(stack(cut(v, c, j)))
        w_col.append(jnp.transpose(jnp.broadcast_to(jnp.exp(last), (LANES, LANES))))
    sc = [_dot_nt(jnp.concatenate([at[u], rt[u]], axis=0), jnp.concatenate([bt[u], kt[u]], axis=0))
          for u in us]
    a_ak = [jnp.where(strict, s[0:c2, c2:2 * c2], 0.0) for s in sc]
    a_rb = [jnp.where(incl, s[c2:2 * c2, 0:c2], 0.0) for s in sc]
    a_rk = [jnp.where(incl, s[c2:2 * c2, c2:2 * c2], 0.0) for s in sc]
    tinv = _unit_lower_inverses([jnp.where(strict, -s[0:c2, 0:c2], 0.0) for s in sc], CHUNK)
    akv = [_dot(a_ak[u], v2[u]) for u in us]
    pu = [_dot(tinv[u], jnp.concatenate([at[u], akv[u]], axis=1)) for u in us]
    mn = [_dot_tn(bd[u], pu[u]) for u in us]
    qy = [_dot(a_rb[u], pu[u]) for u in us]
    n_add = [mn[u][:, LANES:] + _dot_tn(kd[u], v2[u]) for u in us]
    y0 = [qy[u][:, LANES:] + _dot(a_rk[u], v2[u]) for u in us]

    h = [h_ref[j] for j in range(n_pairs)]
    ys = [[] for _ in range(n_pairs)]
    for u, (c, j) in enumerate(units):
        y2 = _dot(rt[u] + qy[u][:, 0:LANES], h[j]) + y0[u]
        h[j] = w_col[u] * h[j] + _dot(mn[u][:, 0:LANES], h[j]) + n_add[u]
        ys[j].append(y2[0:CHUNK, :] + y2[CHUNK:c2, :])
    for j in range(n_pairs):
        h_ref[j] = h[j]

    y = jnp.concatenate([jnp.concatenate(yj, axis=0) for yj in ys], axis=1)
    mu = _dot(y, head_ones) * (1.0 / RWKV_N)
    yc = y - mu
    var = _dot(yc * yc, head_ones) * (1.0 / RWKV_N)
    gn = yc * lax.rsqrt(var + RWKV_GN_EPS) * prow(ROW_LNW) + prow(ROW_LNB)
    bonus = _dot01_right(r * k * prow(ROW_RK), head_ones, 2) * v
    o_ref[...] = (gn + bonus) * gate


def _rwkv_mix(p, rows, mix_lora, w2, a2, g2, tb=256, n_pairs=4):
    t = p.shape[0]
    width = n_pairs * LANES
    pcol = lambda base: pl.BlockSpec((tb, width), lambda h, i: (i, base // width + h))
    wcol = lambda height: pl.BlockSpec((height, width), lambda h, i: (0, h))
    return pl.pallas_call(
        functools.partial(_rwkv_mix_kernel, n_chunks=tb // CHUNK, n_pairs=n_pairs),
        grid=(D // width, t // tb),
        in_specs=[pcol(C_R), pcol(C_K), pcol(C_V),
                  pl.BlockSpec((tb, LORA_W), lambda h, i: (i, C_LORA // LORA_W)),
                  wcol(RWKV_ROWS),
                  pl.BlockSpec((1, LORA_W), lambda h, i: (0, 0)),
                  wcol(128), wcol(128), wcol(256)],
        out_specs=pl.BlockSpec((tb, width), lambda h, i: (i, h)),
        out_shape=jax.ShapeDtypeStruct((t, D), F32),
        scratch_shapes=[pltpu.VMEM((tb + SUBLANES, 3 * width + LORA_W), F32),
                        pltpu.VMEM((n_pairs, LANES, LANES), F32)],
        compiler_params=pltpu.CompilerParams(
            dimension_semantics=("arbitrary", "arbitrary"), vmem_limit_bytes=VMEM_LIMIT),
        name="rwkv_mix",
    )(p, p, p, p, rows, mix_lora, w2, a2, g2)


def _gdn_mix_kernel(pq_ref, pk_ref, pv_ref, ba_ref, z_ref, cq_ref, ck_ref, cv_ref, alog_ref, dtb_ref,
                    nw_ref, o_ref, buf_ref, s_ref, *, n_chunks, n_heads):
    tb = n_chunks * CHUNK
    width = n_heads * LANES
    first_head = pl.program_id(0) * n_heads

    @pl.when(pl.program_id(1) == 0)
    def _():
        buf_ref[0:SUBLANES, :] = jnp.zeros((SUBLANES, 3 * width), F32)
        s_ref[...] = jnp.zeros((n_heads, GDN_N, GDN_N), F32)

    buf_ref[SUBLANES:SUBLANES + tb, 0:width] = pq_ref[...]
    buf_ref[SUBLANES:SUBLANES + tb, width:2 * width] = pk_ref[...]
    buf_ref[SUBLANES:SUBLANES + tb, 2 * width:3 * width] = pv_ref[...]
    cw = jnp.concatenate([cq_ref[...], ck_ref[...], cv_ref[...]], axis=1)
    acc = jnp.zeros((tb, 3 * width), F32)
    for j in range(GDN_CONV):
        off = SUBLANES - (GDN_CONV - 1) + j
        acc = acc + buf_ref[off:off + tb, :] * cw[j:j + 1, :]
    buf_ref[0:SUBLANES, :] = buf_ref[tb:tb + SUBLANES, :]
    qkv = acc * jax.nn.sigmoid(acc)

    def l2n(t):
        return t * lax.rsqrt(jnp.sum(t * t, axis=-1, keepdims=True) + 1e-6)

    ba = ba_ref[...]
    beta_in = jax.nn.sigmoid(ba)
    g_in = -jnp.exp(alog_ref[...]) * jax.nn.softplus(ba + dtb_ref[...])
    sr = _iota2((LANES, LANES), 0)
    q_h, k_h, kb_h, vb_h, gc_h = [], [], [], [], []
    for j in range(n_heads):
        lanes = slice(j * LANES, (j + 1) * LANES)
        head = first_head + j
        beta = _dot01_right(beta_in, jnp.where(sr == head, 1.0, 0.0), 3)
        g = _dot01_right(g_in, jnp.where(sr == head + GDN_HEADS, 1.0, 0.0), 3)
        k_all = l2n(qkv[:, width:2 * width][:, lanes])
        q_h.append(l2n(qkv[:, 0:width][:, lanes]) * (GDN_N ** -0.5))
        k_h.append(k_all)
        kb_h.append(k_all * beta)
        vb_h.append(qkv[:, 2 * width:3 * width][:, lanes] * beta)
        gc_h.append(_chunk_cumsum(g))

    tr = _iota2((CHUNK, CHUNK), 0)
    tc = _iota2((CHUNK, CHUNK), 1)
    causal = tr >= tc
    strict = tr > tc
    units = [(c, j) for c in range(n_chunks) for j in range(n_heads)]
    us = range(len(units))
    cut = lambda per_head, c, j: per_head[j][c * CHUNK:(c + 1) * CHUNK, :]

    q = [cut(q_h, c, j) for c, j in units]
    k = [cut(k_h, c, j) for c, j in units]
    kb = [cut(kb_h, c, j) for c, j in units]
    vb = [cut(vb_h, c, j) for c, j in units]
    gcb = [cut(gc_h, c, j) for c, j in units]
    decay = []
    for u in us:
        g_row = jnp.transpose(gcb[u])[0:CHUNK, :]
        dlog = gcb[u][:, 0:CHUNK] - g_row
        decay.append(jnp.where(causal, jnp.exp(jnp.minimum(dlog, 0.0)), 0.0))
    sc = [_dot_nt(jnp.concatenate([kb[u], q[u]], axis=0), k[u]) for u in us]
    qk = [sc[u][CHUNK:2 * CHUNK, :] * decay[u] for u in us]
    tinv = _unit_lower_inverses(
        [jnp.where(strict, sc[u][0:CHUNK, :] * decay[u], 0.0) for u in us], CHUNK)
    e_gc = [jnp.exp(gg) for gg in gcb]
    g_last = [gg[CHUNK - 1:CHUNK, :] for gg in gcb]
    uw = [_dot(tinv[u], jnp.concatenate([vb[u], kb[u] * e_gc[u]], axis=1)) for u in us]
    mn = [_dot_tn(k[u] * jnp.exp(g_last[u] - gcb[u]), uw[u]) for u in us]
    qq = [_dot(qk[u], uw[u]) for u in us]

    s = [s_ref[j] for j in range(n_heads)]
    os = [[] for _ in range(n_heads)]
    for u, (c, j) in enumerate(units):
        os[j].append(_dot(q[u] * e_gc[u] - qq[u][:, LANES:], s[j]) + qq[u][:, 0:LANES])
        s[j] = jnp.exp(g_last[u]) * s[j] - _dot(mn[u][:, LANES:], s[j]) + mn[u][:, 0:LANES]
    for j in range(n_heads):
        s_ref[j] = s[j]

    z = z_ref[...]
    gated = z * jax.nn.sigmoid(z)
    for j in range(n_heads):
        lanes = slice(j * LANES, (j + 1) * LANES)
        o = jnp.concatenate(os[j], axis=0)
        ms = jnp.mean(o * o, axis=-1, keepdims=True)
        o_ref[:, lanes] = o * lax.rsqrt(ms + NORM_EPS) * nw_ref[...] * gated[:, lanes]


def _gdn_mix(p, conv_w, a_log, dt_bias, norm_w, tb=256, n_heads=4):
    t = p.shape[0]
    width = n_heads * LANES
    pcol = lambda base: pl.BlockSpec((tb, width), lambda h, i: (i, base // width + h))
    ccol = lambda base: pl.BlockSpec((GDN_CONV, width), lambda h, i: (0, base // width + h))
    one = pl.BlockSpec((1, LANES), lambda h, i: (0, 0))
    return pl.pallas_call(
        functools.partial(_gdn_mix_kernel, n_chunks=tb // CHUNK, n_heads=n_heads),
        grid=(D // width, t // tb),
        in_specs=[pcol(C_QKV), pcol(C_QKV + D), pcol(C_QKV + 2 * D),
                  pl.BlockSpec((tb, LANES), lambda h, i: (i, C_BA // LANES)),
                  pcol(C_Z), ccol(0), ccol(D), ccol(2 * D), one, one, one],
        out_specs=pl.BlockSpec((tb, width), lambda h, i: (i, h)),
        out_shape=jax.ShapeDtypeStruct((t, D), F32),
        scratch_shapes=[pltpu.VMEM((tb + SUBLANES, 3 * width), F32),
                        pltpu.VMEM((n_heads, GDN_N, GDN_N), F32)],
        compiler_params=pltpu.CompilerParams(
            dimension_semantics=("arbitrary", "arbitrary"), vmem_limit_bytes=VMEM_LIMIT),
        name="gdn_mix",
    )(p, p, p, p, p, conv_w, conv_w, conv_w, a_log, dt_bias, norm_w)


def _out_proj_kernel(ya_ref, yb_ref, ga_ref, gb_ref, x_ref, w_ref, g_ref, o_ref):
    mixed = jax.nn.sigmoid(ga_ref[...]) * ya_ref[...] + jax.nn.sigmoid(gb_ref[...]) * yb_ref[...]
    y = _dot(mixed.astype(BF16), w_ref[...])
    ms = jnp.mean(y * y, axis=-1, keepdims=True)
    o_ref[...] = x_ref[...] + y * lax.rsqrt(ms + NORM_EPS) * g_ref[...]


def _out_proj(ya, yb, p, x, w_out, gain, tm=256):
    t = x.shape[0]
    blk = pl.BlockSpec((tm, D), lambda i: (i, 0))
    return pl.pallas_call(
        _out_proj_kernel,
        grid=(t // tm,),
        in_specs=[blk, blk,
                  pl.BlockSpec((tm, D), lambda i: (i, C_GA // D)),
                  pl.BlockSpec((tm, D), lambda i: (i, C_GB // D)),
                  blk,
                  pl.BlockSpec((D, D), lambda i: (0, 0)),
                  pl.BlockSpec((1, D), lambda i: (0, 0))],
        out_specs=blk,
        out_shape=jax.ShapeDtypeStruct((t, D), F32),
        compiler_params=pltpu.CompilerParams(
            dimension_semantics=("arbitrary",), vmem_limit_bytes=VMEM_LIMIT),
        name="out_proj",
    )(ya, yb, p, p, x, w_out, gain)


def _ffn_kernel(x_ref, gpre_ref, wg_ref, wu_ref, wd_ref, gpost_ref, o_ref, xn_ref, acc_ref):
    f = pl.program_id(1)

    @pl.when(f == 0)
    def _():
        x = x_ref[...]
        ms = jnp.mean(x * x, axis=-1, keepdims=True)
        xn_ref[...] = (x * lax.rsqrt(ms + NORM_EPS) * gpre_ref[...]).astype(BF16)
        acc_ref[...] = jnp.zeros(acc_ref.shape, F32)

    xn = xn_ref[...]
    gate = _dot(xn, wg_ref[...])
    up = _dot(xn, wu_ref[...])
    hidden = (gate * jax.nn.sigmoid(gate) * up).astype(BF16)
    acc_ref[...] += _dot(hidden, wd_ref[...])

    @pl.when(f == pl.num_programs(1) - 1)
    def _():
        y = acc_ref[...]
        ms = jnp.mean(y * y, axis=-1, keepdims=True)
        o_ref[...] = x_ref[...] + y * lax.rsqrt(ms + NORM_EPS) * gpost_ref[...]


def _ffn(x, g_pre, w_gate, w_up, w_down, g_post, tm=512, tf=512):
    t = x.shape[0]
    return pl.pallas_call(
        _ffn_kernel,
        grid=(t // tm, D_FF // tf),
        in_specs=[pl.BlockSpec((tm, D), lambda i, f: (i, 0)),
                  pl.BlockSpec((1, D), lambda i, f: (0, 0)),
                  pl.BlockSpec((D, tf), lambda i, f: (0, f)),
                  pl.BlockSpec((D, tf), lambda i, f: (0, f)),
                  pl.BlockSpec((tf, D), lambda i, f: (f, 0)),
                  pl.BlockSpec((1, D), lambda i, f: (0, 0))],
        out_specs=pl.BlockSpec((tm, D), lambda i, f: (i, 0)),
        out_shape=jax.ShapeDtypeStruct((t, D), F32),
        scratch_shapes=[pltpu.VMEM((tm, D), BF16), pltpu.VMEM((tm, D), F32)],
        compiler_params=pltpu.CompilerParams(
            dimension_semantics=("arbitrary", "arbitrary"), vmem_limit_bytes=VMEM_LIMIT),
        name="ffn",
    )(x, g_pre, w_gate, w_up, w_down, g_post)


def _pad_cols(w, width):
    return jnp.pad(w, ((0, 0), (0, width - w.shape[1])))


def _pad_rows(w, height):
    return jnp.pad(w, ((0, height - w.shape[0]), (0, 0)))


def _layer(x, norm_mix_pre, norm_mix_post, norm_ffn_pre, norm_ffn_post, w_in, rwkv_mix, rwkv_w0,
           rwkv_w2, rwkv_a0, rwkv_a2, rwkv_g2, rwkv_k_k, rwkv_k_a, rwkv_r_k, rwkv_ln_w, rwkv_ln_b,
           gdn_conv_w, gdn_a_log, gdn_dt_bias, gdn_norm_w, w_out, ffn_w_gate, ffn_w_up, ffn_w_down):
    row = lambda v: v.reshape(1, -1).astype(F32)
    o_lw = 3 * D
    o_la = o_lw + 96
    o_lg = o_la + 96
    o_qkv = o_lg + 256
    o_z = o_qkv + 3 * D
    o_beta = o_z + D
    o_ga = o_beta + 2 * GDN_HEADS
    o_gb = o_ga + D
    cols = lambda lo, hi: w_in[:, lo:hi].astype(BF16)
    w_all = jnp.concatenate([
        cols(0, 3 * D), cols(o_qkv, o_z), cols(o_z, o_beta), cols(o_ga, o_gb), cols(o_gb, o_gb + D),
        _pad_cols(cols(o_lw, o_la), 128), _pad_cols(cols(o_la, o_lg), 128), cols(o_lg, o_qkv),
        _pad_cols(cols(o_beta, o_ga), P_COLS - C_BA),
    ], axis=1)
    mix_lora = jnp.concatenate([
        _pad_cols(row(rwkv_mix[o_lw:o_la]), 128), _pad_cols(row(rwkv_mix[o_la:o_lg]), 128),
        row(rwkv_mix[o_lg:o_qkv])], axis=1)
    rwkv_rows = _pad_rows(jnp.concatenate([
        row(rwkv_mix[0:D]), row(rwkv_mix[D:2 * D]), row(rwkv_mix[2 * D:3 * D]), row(rwkv_w0),
        row(rwkv_a0), row(rwkv_k_k), row(rwkv_k_a), row(rwkv_r_k), row(rwkv_ln_w), row(rwkv_ln_b),
    ], axis=0), RWKV_ROWS)
    lane_pad = lambda v: jnp.pad(row(v), ((0, 0), (GDN_HEADS, LANES - 2 * GDN_HEADS)))

    p = _in_proj(x, row(norm_mix_pre), w_all)
    y_a = _rwkv_mix(p, rwkv_rows, mix_lora, _pad_rows(rwkv_w2, 128).astype(BF16),
                    _pad_rows(rwkv_a2, 128).astype(BF16), rwkv_g2.astype(BF16))
    y_b = _gdn_mix(p, gdn_conv_w, lane_pad(gdn_a_log), lane_pad(gdn_dt_bias), row(gdn_norm_w))
    x1 = _out_proj(y_a, y_b, p, x, w_out.astype(BF16), row(norm_mix_post))
    return _ffn(x1, row(norm_ffn_pre), ffn_w_gate.astype(BF16), ffn_w_up.astype(BF16),
                ffn_w_down.astype(BF16), row(norm_ffn_post))


def kernel(x, norm_mix_pre, norm_mix_post, norm_ffn_pre, norm_ffn_post, w_in, rwkv_mix, rwkv_w0,
           rwkv_w2, rwkv_a0, rwkv_a2, rwkv_g2, rwkv_k_k, rwkv_k_a, rwkv_r_k, rwkv_ln_w, rwkv_ln_b,
           gdn_conv_w, gdn_a_log, gdn_dt_bias, gdn_norm_w, w_out, ffn_w_gate, ffn_w_up, ffn_w_down):
    bsz, seq, d = x.shape
    assert (bsz, seq, d) == (1, SEQ, D)
    h = x[0]
    weights = (norm_mix_pre, norm_mix_post, norm_ffn_pre, norm_ffn_post, w_in, rwkv_mix, rwkv_w0,
               rwkv_w2, rwkv_a0, rwkv_a2, rwkv_g2, rwkv_k_k, rwkv_k_a, rwkv_r_k, rwkv_ln_w,
               rwkv_ln_b, gdn_conv_w, gdn_a_log, gdn_dt_bias, gdn_norm_w, w_out, ffn_w_gate,
               ffn_w_up, ffn_w_down)
    for layer in range(norm_mix_pre.shape[0]):
        h = _layer(h, *(w[layer] for w in weights))
    return h[None]
```

```python
import functools

import jax
import jax.numpy as jnp
from jax import lax
from jax.experimental import pallas as pl
from jax.experimental.pallas import tpu as pltpu

F32 = jnp.float32
BF16 = jnp.bfloat16

D = 2048
SEQ = 8192
CHUNK = 64
NORM_EPS = 1e-6
RWKV_N = 64
RWKV_GN_EPS = 64e-5
GDN_N = 128
GDN_HEADS = D // GDN_N
GDN_CONV = 4
D_FF = 5632
LANES = 128
SUBLANES = 8
VMEM_LIMIT = 56 * 1024 * 1024

C_R, C_K, C_V = 0, 2048, 4096
C_QKV = 6144
C_Z = 12288
C_GA = 14336
C_GB = 16384
C_LORA = 18432
C_BA = 18944
P_COLS = 19456
LORA_W = 512
(ROW_MIX_R, ROW_MIX_K, ROW_MIX_V, ROW_W0, ROW_A0, ROW_KK, ROW_KA, ROW_RK, ROW_LNW, ROW_LNB) = range(10)
RWKV_ROWS = 16


def _dot(a, b):
    return jnp.dot(a.astype(BF16), b.astype(BF16), preferred_element_type=F32)


def _dot_nt(a, b):
    return lax.dot_general(a.astype(BF16), b.astype(BF16), (((1,), (1,)), ((), ())),
                           preferred_element_type=F32)


def _dot_tn(a, b):
    return _dot(jnp.transpose(a), b)


def _split_bf16(x, passes):
    parts = []
    for _ in range(passes - 1):
        hi = x.astype(BF16)
        parts.append(hi)
        x = x - hi.astype(F32)
    parts.append(x.astype(BF16))
    return parts


def _dot01_right(x, m01, passes):
    m = m01.astype(BF16)
    return sum(jnp.dot(p, m, preferred_element_type=F32) for p in _split_bf16(x, passes))


def _dot01_left(m01, x, passes):
    m = m01.astype(BF16)
    return sum(jnp.dot(m, p, preferred_element_type=F32) for p in _split_bf16(x, passes))


def _iota2(shape, dim):
    return lax.broadcasted_iota(jnp.int32, shape, dim)


def _chunk_cumsum(x):
    n = 2 * CHUNK
    row = _iota2((n, n), 0)
    col = _iota2((n, n), 1)
    tril = jnp.where(((row // CHUNK) == (col // CHUNK)) & (row >= col), 1.0, 0.0)
    groups = [_dot01_left(tril, x[g:g + n, :], 3) for g in range(0, x.shape[0], n)]
    return jnp.concatenate(groups, axis=0)


def _unit_lower_inverses(lows, n_block):
    n = lows[0].shape[0]
    row = _iota2((n, n), 0)
    col = _iota2((n, n), 1)
    eye = jnp.where(row == col, 1.0, 0.0).astype(F32)
    xs = [eye] * len(lows)
    b = 1
    while b < n_block:
        sel = ((row // (2 * b)) == (col // (2 * b))) & ((row // b) % 2 == 1) & ((col // b) % 2 == 0)
        cs = [jnp.where(sel, low, 0.0) for low in lows]
        if b == 1:
            xs = [x - c for x, c in zip(xs, cs)]
        else:
            xc = [_dot(x, c) for x, c in zip(xs, cs)]
            xs = [x - _dot(t, x) for x, t in zip(xs, xc)]
        b *= 2
    return xs


def _in_proj_kernel(x_ref, g_ref, w_ref, o_ref, xn_ref):
    @pl.when(pl.program_id(1) == 0)
    def _():
        x = x_ref[...]
        ms = jnp.mean(x * x, axis=-1, keepdims=True)
        xn_ref[...] = (x * lax.rsqrt(ms + NORM_EPS) * g_ref[...]).astype(BF16)

    o_ref[...] = _dot(xn_ref[...], w_ref[...])


def _in_proj(x, gain, w_all, tm=1024, tn=1024):
    t = x.shape[0]
    return pl.pallas_call(
        _in_proj_kernel,
        grid=(t // tm, P_COLS // tn),
        in_specs=[
            pl.BlockSpec((tm, D), lambda i, j: (i, 0)),
            pl.BlockSpec((1, D), lambda i, j: (0, 0)),
            pl.BlockSpec((D, tn), lambda i, j: (0, j)),
        ],
        out_specs=pl.BlockSpec((tm, tn), lambda i, j: (i, j)),
        out_shape=jax.ShapeDtypeStruct((t, P_COLS), F32),
        scratch_shapes=[pltpu.VMEM((tm, D), BF16)],
        compiler_params=pltpu.CompilerParams(
            dimension_semantics=("arbitrary", "arbitrary"), vmem_limit_bytes=VMEM_LIMIT),
        name="in_proj",
    )(x, gain, w_all)


def _rwkv_mix_kernel(pr_ref, pk_ref, pv_ref, plora_ref, ga_ref, rows_ref, mixl_ref, w2_ref, a2_ref,
                     g2_ref, o_ref, buf_ref, h_ref, *, n_chunks, n_pairs):
    c2 = 2 * CHUNK
    tb = n_chunks * CHUNK
    width = n_pairs * LANES
    lo_k, lo_v, lo_l = width, 2 * width, 3 * width

    @pl.when(pl.program_id(1) == 0)
    def _():
        buf_ref[0:SUBLANES, :] = jnp.zeros((SUBLANES, lo_l + LORA_W), F32)
        h_ref[...] = jnp.zeros((n_pairs, LANES, LANES), F32)

    cur = jnp.concatenate([pr_ref[...], pk_ref[...], pv_ref[...], plora_ref[...]], axis=1)
    prev = pltpu.roll(jnp.concatenate([buf_ref[...], cur], axis=0), 1, 0)[SUBLANES:, :]
    buf_ref[...] = cur[tb - SUBLANES:tb, :]
    par = rows_ref[...]
    prow = lambda j: par[j:j + 1, :]
    mix = jnp.concatenate([prow(ROW_MIX_R), prow(ROW_MIX_K), prow(ROW_MIX_V), mixl_ref[...]], axis=1)
    xs = cur + (prev - cur) * mix
    r, k, v = xs[:, 0:lo_k], xs[:, lo_k:lo_v], xs[:, lo_v:lo_l]
    lw_in, la_in, lg_in = xs[:, lo_l:lo_l + 128], xs[:, lo_l + 128:lo_l + 256], xs[:, lo_l + 256:]

    gr = _iota2((width, width), 0)
    gc = _iota2((width, width), 1)
    head_ones = jnp.where((gr // RWKV_N) == (gc // RWKV_N), 1.0, 0.0)

    w_log = -jax.nn.softplus(-(prow(ROW_W0) + _dot(jnp.tanh(lw_in), w2_ref[...]))) - 0.5
    lw_all = -jnp.exp(w_log)
    a = jax.nn.sigmoid(prow(ROW_A0) + _dot(la_in, a2_ref[...]))
    gate = _dot(jax.nn.sigmoid(lg_in), g2_ref[...])
    kk = k * prow(ROW_KK)
    kk = kk * lax.rsqrt(_dot01_right(kk * kk, head_ones, 2) + 1e-6)
    k = k * (1.0 + (a - 1.0) * prow(ROW_KA))
    an_all = -kk
    b_all = kk * a
    cw_all = _chunk_cumsum(lw_all)

    lane = _iota2((CHUNK, LANES), 1)
    head0 = lane < RWKV_N
    row = _iota2((c2, c2), 0)
    col = _iota2((c2, c2), 1)
    same = (row // CHUNK) == (col // CHUNK)
    strict = same & (row > col)
    incl = same & (row >= col)

    def stack(t):
        return jnp.concatenate([jnp.where(head0, t, 0.0), jnp.where(head0, 0.0, t)], axis=0)

    units = [(c, j) for c in range(n_chunks) for j in range(n_pairs)]
    us = range(len(units))
    cut = lambda t, c, j: t[c * CHUNK:(c + 1) * CHUNK, j * LANES:(j + 1) * LANES]

    at, rt, bt, kt, bd, kd, v2, w_col = [], [], [], [], [], [], [], []
    for c, j in units:
        cw = cut(cw_all, c, j)
        last = cw[CHUNK - 1:CHUNK, :]
        e_out = jnp.exp(-cw)
        e_end = jnp.exp(last - cw)
        at.append(stack(cut(an_all, c, j) * jnp.exp(cw - cut(lw_all, c, j))))
        rt.append(stack(cut(r, c, j) * jnp.exp(cw)))
        bt.append(stack(cut(b_all, c, j) * e_out))
        kt.append(stack(cut(k, c, j) * e_out))
        bd.append(stack(cut(b_all, c, j) * e_end))
        kd.append(stack(cut(k, c, j) * e_end))
        v2.append(stack(cut(v, c, j)))
        w_col.append(jnp.transpose(jnp.broadcast_to(jnp.exp(last), (LANES, LANES))))
    sc = [_dot_nt(jnp.concatenate([at[u], rt[u]], axis=0), jnp.concatenate([bt[u], kt[u]], axis=0))
          for u in us]
    a_ak = [jnp.where(strict, s[0:c2, c2:2 * c2], 0.0) for s in sc]
    a_rb = [jnp.where(incl, s[c2:2 * c2, 0:c2], 0.0) for s in sc]
    a_rk = [jnp.where(incl, s[c2:2 * c2, c2:2 * c2], 0.0) for s in sc]
    tinv = _unit_lower_inverses([jnp.where(strict, -s[0:c2, 0:c2], 0.0) for s in sc], CHUNK)
    akv = [_dot(a_ak[u], v2[u]) for u in us]
    pu = [_dot(tinv[u], jnp.concatenate([at[u], akv[u]], axis=1)) for u in us]
    mn = [_dot_tn(bd[u], pu[u]) for u in us]
    qy = [_dot(a_rb[u], pu[u]) for u in us]
    n_add = [mn[u][:, LANES:] + _dot_tn(kd[u], v2[u]) for u in us]
    y0 = [qy[u][:, LANES:] + _dot(a_rk[u], v2[u]) for u in us]

    h = [h_ref[j] for j in range(n_pairs)]
    ys = [[] for _ in range(n_pairs)]
    for u, (c, j) in enumerate(units):
        y2 = _dot(rt[u] + qy[u][:, 0:LANES], h[j]) + y0[u]
        h[j] = w_col[u] * h[j] + _dot(mn[u][:, 0:LANES], h[j]) + n_add[u]
        ys[j].append(y2[0:CHUNK, :] + y2[CHUNK:c2, :])
    for j in range(n_pairs):
        h_ref[j] = h[j]

    y = jnp.concatenate([jnp.concatenate(yj, axis=0) for yj in ys], axis=1)
    mu = _dot(y, head_ones) * (1.0 / RWKV_N)
    yc = y - mu
    var = _dot(yc * yc, head_ones) * (1.0 / RWKV_N)
    gn = yc * lax.rsqrt(var + RWKV_GN_EPS) * prow(ROW_LNW) + prow(ROW_LNB)
    bonus = _dot01_right(r * k * prow(ROW_RK), head_ones, 2) * v
    o_ref[...] = jax.nn.sigmoid(ga_ref[...]) * ((gn + bonus) * gate)


def _rwkv_mix(p, rows, mix_lora, w2, a2, g2, tb=256, n_pairs=4):
    t = p.shape[0]
    width = n_pairs * LANES
    pcol = lambda base: pl.BlockSpec((tb, width), lambda h, i: (i, base // width + h))
    wcol = lambda height: pl.BlockSpec((height, width), lambda h, i: (0, h))
    return pl.pallas_call(
        functools.partial(_rwkv_mix_kernel, n_chunks=tb // CHUNK, n_pairs=n_pairs),
        grid=(D // width, t // tb),
        in_specs=[pcol(C_R), pcol(C_K), pcol(C_V),
                  pl.BlockSpec((tb, LORA_W), lambda h, i: (i, C_LORA // LORA_W)),
                  pcol(C_GA), wcol(RWKV_ROWS),
                  pl.BlockSpec((1, LORA_W), lambda h, i: (0, 0)),
                  wcol(128), wcol(128), wcol(256)],
        out_specs=pl.BlockSpec((tb, width), lambda h, i: (i, h)),
        out_shape=jax.ShapeDtypeStruct((t, D), F32),
        scratch_shapes=[pltpu.VMEM((SUBLANES, 3 * width + LORA_W), F32),
                        pltpu.VMEM((n_pairs, LANES, LANES), F32)],
        compiler_params=pltpu.CompilerParams(
            dimension_semantics=("arbitrary", "arbitrary"), vmem_limit_bytes=VMEM_LIMIT),
        name="rwkv_mix",
    )(p, p, p, p, p, rows, mix_lora, w2, a2, g2)


def _gdn_mix_kernel(pq_ref, pk_ref, pv_ref, ba_ref, z_ref, gb_ref, ya_ref, cq_ref, ck_ref, cv_ref,
                    alog_ref, dtb_ref, nw_ref, o_ref, buf_ref, s_ref, *, n_chunks, n_heads):
    tb = n_chunks * CHUNK
    width = n_heads * LANES
    first_head = pl.program_id(0) * n_heads

    @pl.when(pl.program_id(1) == 0)
    def _():
        buf_ref[0:SUBLANES, :] = jnp.zeros((SUBLANES, 3 * width), F32)
        s_ref[...] = jnp.zeros((n_heads, GDN_N, GDN_N), F32)

    xs = jnp.concatenate(
        [buf_ref[...], jnp.concatenate([pq_ref[...], pk_ref[...], pv_ref[...]], axis=1)], axis=0)
    buf_ref[...] = xs[tb:tb + SUBLANES, :]
    cw = jnp.concatenate([cq_ref[...], ck_ref[...], cv_ref[...]], axis=1)
    acc = xs[SUBLANES:, :] * cw[GDN_CONV - 1:GDN_CONV, :]
    for j in range(GDN_CONV - 2, -1, -1):
        xs = pltpu.roll(xs, 1, 0)
        acc = acc + xs[SUBLANES:, :] * cw[j:j + 1, :]
    qkv = acc * jax.nn.sigmoid(acc)

    def l2n(t):
        return t * lax.rsqrt(jnp.sum(t * t, axis=-1, keepdims=True) + 1e-6)

    ba = ba_ref[...]
    beta_in = jax.nn.sigmoid(ba)
    g_in = -jnp.exp(alog_ref[...]) * jax.nn.softplus(ba + dtb_ref[...])
    sr = _iota2((LANES, LANES), 0)
    q_h, k_h, kb_h, vb_h, gc_h = [], [], [], [], []
    for j in range(n_heads):
        lanes = slice(j * LANES, (j + 1) * LANES)
        head = first_head + j
        beta = _dot01_right(beta_in, jnp.where(sr == head, 1.0, 0.0), 3)
        g = _dot01_right(g_in, jnp.where(sr == head + GDN_HEADS, 1.0, 0.0), 3)
        k_all = l2n(qkv[:, width:2 * width][:, lanes])
        q_h.append(l2n(qkv[:, 0:width][:, lanes]) * (GDN_N ** -0.5))
        k_h.append(k_all)
        kb_h.append(k_all * beta)
        vb_h.append(qkv[:, 2 * width:3 * width][:, lanes] * beta)
        gc_h.append(_chunk_cumsum(g))

    tr = _iota2((CHUNK, CHUNK), 0)
    tc = _iota2((CHUNK, CHUNK), 1)
    causal = tr >= tc
    strict = tr > tc
    units = [(c, j) for c in range(n_chunks) for j in range(n_heads)]
    us = range(len(units))
    cut = lambda per_head, c, j: per_head[j][c * CHUNK:(c + 1) * CHUNK, :]

    q = [cut(q_h, c, j) for c, j in units]
    k = [cut(k_h, c, j) for c, j in units]
    kb = [cut(kb_h, c, j) for c, j in units]
    vb = [cut(vb_h, c, j) for c, j in units]
    gcb = [cut(gc_h, c, j) for c, j in units]
    decay = []
    for u in us:
        g_row = jnp.transpose(gcb[u])[0:CHUNK, :]
        dlog = gcb[u][:, 0:CHUNK] - g_row
        decay.append(jnp.where(causal, jnp.exp(jnp.minimum(dlog, 0.0)), 0.0))
    sc = [_dot_nt(jnp.concatenate([kb[u], q[u]], axis=0), k[u]) for u in us]
    qk = [sc[u][CHUNK:2 * CHUNK, :] * decay[u] for u in us]
    tinv = _unit_lower_inverses(
        [jnp.where(strict, sc[u][0:CHUNK, :] * decay[u], 0.0) for u in us], CHUNK)
    e_gc = [jnp.exp(gg) for gg in gcb]
    g_last = [gg[CHUNK - 1:CHUNK, :] for gg in gcb]
    uw = [_dot(tinv[u], jnp.concatenate([vb[u], kb[u] * e_gc[u]], axis=1)) for u in us]
    mn = [_dot_tn(k[u] * jnp.exp(g_last[u] - gcb[u]), uw[u]) for u in us]
    qq = [_dot(qk[u], uw[u]) for u in us]

    s = [s_ref[j] for j in range(n_heads)]
    os = [[] for _ in range(n_heads)]
    for u, (c, j) in enumerate(units):
        os[j].append(_dot(q[u] * e_gc[u] - qq[u][:, LANES:], s[j]) + qq[u][:, 0:LANES])
        s[j] = jnp.exp(g_last[u]) * s[j] - _dot(mn[u][:, LANES:], s[j]) + mn[u][:, 0:LANES]
    for j in range(n_heads):
        s_ref[j] = s[j]

    z = z_ref[...]
    gated = z * jax.nn.sigmoid(z) * jax.nn.sigmoid(gb_ref[...])
    ya = ya_ref[...]
    for j in range(n_heads):
        lanes = slice(j * LANES, (j + 1) * LANES)
        o = jnp.concatenate(os[j], axis=0)
        ms = jnp.mean(o * o, axis=-1, keepdims=True)
        y_b = o * lax.rsqrt(ms + NORM_EPS) * nw_ref[...] * gated[:, lanes]
        o_ref[:, lanes] = (ya[:, lanes] + y_b).astype(BF16)


def _gdn_mix(p, y_a, conv_w, a_log, dt_bias, norm_w, tb=512, n_heads=4):
    t = p.shape[0]
    width = n_heads * LANES
    pcol = lambda base: pl.BlockSpec((tb, width), lambda h, i: (i, base // width + h))
    ccol = lambda base: pl.BlockSpec((GDN_CONV, width), lambda h, i: (0, base // width + h))
    one = pl.BlockSpec((1, LANES), lambda h, i: (0, 0))
    return pl.pallas_call(
        functools.partial(_gdn_mix_kernel, n_chunks=tb // CHUNK, n_heads=n_heads),
        grid=(D // width, t // tb),
        in_specs=[pcol(C_QKV), pcol(C_QKV + D), pcol(C_QKV + 2 * D),
                  pl.BlockSpec((tb, LANES), lambda h, i: (i, C_BA // LANES)),
                  pcol(C_Z), pcol(C_GB), pl.BlockSpec((tb, width), lambda h, i: (i, h)),
                  ccol(0), ccol(D), ccol(2 * D), one, one, one],
        out_specs=pl.BlockSpec((tb, width), lambda h, i: (i, h)),
        out_shape=jax.ShapeDtypeStruct((t, D), BF16),
        scratch_shapes=[pltpu.VMEM((SUBLANES, 3 * width), F32),
                        pltpu.VMEM((n_heads, GDN_N, GDN_N), F32)],
        compiler_params=pltpu.CompilerParams(
            dimension_semantics=("arbitrary", "arbitrary"), vmem_limit_bytes=VMEM_LIMIT),
        name="gdn_mix",
    )(p, p, p, p, p, p, y_a, conv_w, conv_w, conv_w, a_log, dt_bias, norm_w)


def _out_proj_kernel(mixed_ref, x_ref, w_ref, g_ref, o_ref):
    y = _dot(mixed_ref[...], w_ref[...])
    ms = jnp.mean(y * y, axis=-1, keepdims=True)
    o_ref[...] = x_ref[...] + y * lax.rsqrt(ms + NORM_EPS) * g_ref[...]


def _out_proj(mixed, x, w_out, gain, tm=512):
    t = x.shape[0]
    blk = pl.BlockSpec((tm, D), lambda i: (i, 0))
    return pl.pallas_call(
        _out_proj_kernel,
        grid=(t // tm,),
        in_specs=[blk, blk,
                  pl.BlockSpec((D, D), lambda i: (0, 0)),
                  pl.BlockSpec((1, D), lambda i: (0, 0))],
        out_specs=blk,
        out_shape=jax.ShapeDtypeStruct((t, D), F32),
        compiler_params=pltpu.CompilerParams(
            dimension_semantics=("arbitrary",), vmem_limit_bytes=VMEM_LIMIT),
        name="out_proj",
    )(mixed, x, w_out, gain)


def _ffn_kernel(x_ref, gpre_ref, wg_ref, wu_ref, wd_ref, gpost_ref, o_ref, xn_ref, acc_ref):
    f = pl.program_id(1)

    @pl.when(f == 0)
    def _():
        x = x_ref[...]
        ms = jnp.mean(x * x, axis=-1, keepdims=True)
        xn_ref[...] = (x * lax.rsqrt(ms + NORM_EPS) * gpre_ref[...]).astype(BF16)
        acc_ref[...] = jnp.zeros(acc_ref.shape, F32)

    xn = xn_ref[...]
    gate = _dot(xn, wg_ref[...])
    up = _dot(xn, wu_ref[...])
    hidden = (gate * jax.nn.sigmoid(gate) * up).astype(BF16)
    acc_ref[...] += _dot(hidden, wd_ref[...])

    @pl.when(f == pl.num_programs(1) - 1)
    def _():
        y = acc_ref[...]
        ms = jnp.mean(y * y, axis=-1, keepdims=True)
        o_ref[...] = x_ref[...] + y * lax.rsqrt(ms + NORM_EPS) * gpost_ref[...]


def _ffn(x, g_pre, w_gate, w_up, w_down, g_post, tm=512, tf=512):
    t = x.shape[0]
    return pl.pallas_call(
        _ffn_kernel,
        grid=(t // tm, D_FF // tf),
        in_specs=[pl.BlockSpec((tm, D), lambda i, f: (i, 0)),
                  pl.BlockSpec((1, D), lambda i, f: (0, 0)),
                  pl.BlockSpec((D, tf), lambda i, f: (0, f)),
                  pl.BlockSpec((D, tf), lambda i, f: (0, f)),
                  pl.BlockSpec((tf, D), lambda i, f: (f, 0)),
                  pl.BlockSpec((1, D), lambda i, f: (0, 0))],
        out_specs=pl.BlockSpec((tm, D), lambda i, f: (i, 0)),
        out_shape=jax.ShapeDtypeStruct((t, D), F32),
        scratch_shapes=[pltpu.VMEM((tm, D), BF16), pltpu.VMEM((tm, D), F32)],
        compiler_params=pltpu.CompilerParams(
            dimension_semantics=("arbitrary", "arbitrary"), vmem_limit_bytes=VMEM_LIMIT),
        name="ffn",
    )(x, g_pre, w_gate, w_up, w_down, g_post)


def _pad_cols(w, width):
    return jnp.pad(w, ((0, 0), (0, width - w.shape[1])))


def _pad_rows(w, height):
    return jnp.pad(w, ((0, height - w.shape[0]), (0, 0)))


def _layer(x, norm_mix_pre, norm_mix_post, norm_ffn_pre, norm_ffn_post, w_in, rwkv_mix, rwkv_w0,
           rwkv_w2, rwkv_a0, rwkv_a2, rwkv_g2, rwkv_k_k, rwkv_k_a, rwkv_r_k, rwkv_ln_w, rwkv_ln_b,
           gdn_conv_w, gdn_a_log, gdn_dt_bias, gdn_norm_w, w_out, ffn_w_gate, ffn_w_up, ffn_w_down):
    row = lambda v: v.reshape(1, -1).astype(F32)
    o_lw = 3 * D
    o_la = o_lw + 96
    o_lg = o_la + 96
    o_qkv = o_lg + 256
    o_z = o_qkv + 3 * D
    o_beta = o_z + D
    o_ga = o_beta + 2 * GDN_HEADS
    o_gb = o_ga + D
    cols = lambda lo, hi: w_in[:, lo:hi].astype(BF16)
    w_all = jnp.concatenate([
        cols(0, 3 * D), cols(o_qkv, o_z), cols(o_z, o_beta), cols(o_ga, o_gb), cols(o_gb, o_gb + D),
        _pad_cols(cols(o_lw, o_la), 128), _pad_cols(cols(o_la, o_lg), 128), cols(o_lg, o_qkv),
        _pad_cols(cols(o_beta, o_ga), P_COLS - C_BA),
    ], axis=1)
    mix_lora = jnp.concatenate([
        _pad_cols(row(rwkv_mix[o_lw:o_la]), 128), _pad_cols(row(rwkv_mix[o_la:o_lg]), 128),
        row(rwkv_mix[o_lg:o_qkv])], axis=1)
    rwkv_rows = _pad_rows(jnp.concatenate([
        row(rwkv_mix[0:D]), row(rwkv_mix[D:2 * D]), row(rwkv_mix[2 * D:3 * D]), row(rwkv_w0),
        row(rwkv_a0), row(rwkv_k_k), row(rwkv_k_a), row(rwkv_r_k), row(rwkv_ln_w), row(rwkv_ln_b),
    ], axis=0), RWKV_ROWS)
    lane_pad = lambda v: jnp.pad(row(v), ((0, 0), (GDN_HEADS, LANES - 2 * GDN_HEADS)))

    p = _in_proj(x, row(norm_mix_pre), w_all)
    y_a = _rwkv_mix(p, rwkv_rows, mix_lora, _pad_rows(rwkv_w2, 128).astype(BF16),
                    _pad_rows(rwkv_a2, 128).astype(BF16), rwkv_g2.astype(BF16))
    mixed = _gdn_mix(p, y_a, gdn_conv_w, lane_pad(gdn_a_log), lane_pad(gdn_dt_bias),
                     row(gdn_norm_w))
    x1 = _out_proj(mixed, x, w_out.astype(BF16), row(norm_mix_post))
    return _ffn(x1, row(norm_ffn_pre), ffn_w_gate.astype(BF16), ffn_w_up.astype(BF16),
                ffn_w_down.astype(BF16), row(norm_ffn_post))


def kernel(x, norm_mix_pre, norm_mix_post, norm_ffn_pre, norm_ffn_post, w_in, rwkv_mix, rwkv_w0,
           rwkv_w2, rwkv_a0, rwkv_a2, rwkv_g2, rwkv_k_k, rwkv_k_a, rwkv_r_k, rwkv_ln_w, rwkv_ln_b,
           gdn_conv_w, gdn_a_log, gdn_dt_bias, gdn_norm_w, w_out, ffn_w_gate, ffn_w_up, ffn_w_down):
    bsz, seq, d = x.shape
    assert (bsz, seq, d) == (1, SEQ, D)
    h = x[0]
    weights = (norm_mix_pre, norm_mix_post, norm_ffn_pre, norm_ffn_post, w_in, rwkv_mix, rwkv_w0,
               rwkv_w2, rwkv_a0, rwkv_a2, rwkv_g2, rwkv_k_k, rwkv_k_a, rwkv_r_k, rwkv_ln_w,
               rwkv_ln_b, gdn_conv_w, gdn_a_log, gdn_dt_bias, gdn_norm_w, w_out, ffn_w_gate,
               ffn_w_up, ffn_w_down)
    for layer in range(norm_mix_pre.shape[0]):
        h = _layer(h, *(w[layer] for w in weights))
    return h[None]
```

```python
import functools

import jax
import jax.numpy as jnp
from jax import lax
from jax.experimental import pallas as pl
from jax.experimental.pallas import tpu as pltpu

F32 = jnp.float32
BF16 = jnp.bfloat16

D = 2048
SEQ = 8192
CHUNK = 64
NORM_EPS = 1e-6
RWKV_N = 64
RWKV_GN_EPS = 64e-5
GDN_N = 128
GDN_HEADS = D // GDN_N
GDN_CONV = 4
D_FF = 5632
LANES = 128
SUBLANES = 8
MXU_DIM = 256
VMEM_LIMIT = 56 * 1024 * 1024

C_R, C_K, C_V = 0, 2048, 4096
C_QKV = 6144
C_Z = 12288
C_GA = 14336
C_GB = 16384
C_LORA = 18432
C_BA = 18944
P_COLS = 19456
LORA_W = 512
(ROW_MIX_R, ROW_MIX_K, ROW_MIX_V, ROW_W0, ROW_A0, ROW_KK, ROW_KA, ROW_RK, ROW_LNW, ROW_LNB) = range(10)
RWKV_ROWS = 16


def _dot(a, b):
    return jnp.dot(a.astype(BF16), b.astype(BF16), preferred_element_type=F32)


def _dot_nt(a, b):
    return lax.dot_general(a.astype(BF16), b.astype(BF16), (((1,), (1,)), ((), ())),
                           preferred_element_type=F32)


def _split_bf16(x, passes):
    parts = []
    for _ in range(passes - 1):
        hi = x.astype(BF16)
        parts.append(hi)
        x = x - hi.astype(F32)
    parts.append(x.astype(BF16))
    return parts


def _dot01_right(x, m01, passes):
    m = m01.astype(BF16)
    return sum(jnp.dot(p, m, preferred_element_type=F32) for p in _split_bf16(x, passes))


def _dot01_left(m01, x, passes):
    m = m01.astype(BF16)
    return sum(jnp.dot(m, p, preferred_element_type=F32) for p in _split_bf16(x, passes))


def _iota2(shape, dim):
    return lax.broadcasted_iota(jnp.int32, shape, dim)


def _chunk_cumsum(x):
    n = 2 * CHUNK
    row = _iota2((n, n), 0)
    col = _iota2((n, n), 1)
    tril = jnp.where(((row // CHUNK) == (col // CHUNK)) & (row >= col), 1.0, 0.0)
    groups = [_dot01_left(tril, x[g:g + n, :], 3) for g in range(0, x.shape[0], n)]
    return jnp.concatenate(groups, axis=0)


def _head_sums(x, passes):
    gr = _iota2((MXU_DIM, MXU_DIM), 0)
    gc = _iota2((MXU_DIM, MXU_DIM), 1)
    ones = jnp.where((gr // RWKV_N) == (gc // RWKV_N), 1.0, 0.0)
    groups = [_dot01_right(x[:, g:g + MXU_DIM], ones, passes) for g in range(0, x.shape[1], MXU_DIM)]
    return jnp.concatenate(groups, axis=1)


def _unit_lower_inverses(lows, n_block):
    n = lows[0].shape[0]
    row = _iota2((n, n), 0)
    col = _iota2((n, n), 1)
    eye = jnp.where(row == col, 1.0, 0.0).astype(F32)
    xs = [eye] * len(lows)
    b = 1
    while b < n_block:
        sel = ((row // (2 * b)) == (col // (2 * b))) & ((row // b) % 2 == 1) & ((col // b) % 2 == 0)
        cs = [jnp.where(sel, low, 0.0) for low in lows]
        if b == 1:
            xs = [x - c for x, c in zip(xs, cs)]
        else:
            xc = []
            for x, c in zip(xs, cs):
                xc.append(_dot(x, c))
                yield
            nxt = []
            for x, t in zip(xs, xc):
                nxt.append(x - _dot(t, x))
                yield
            xs = nxt
        b *= 2
    return xs


def _interleave(*streams):
    live = list(streams)
    while live:
        for s in list(live):
            try:
                next(s)
            except StopIteration:
                live.remove(s)


def _in_proj_kernel(x_ref, g_ref, w_ref, o_ref, xn_ref):
    @pl.when(pl.program_id(1) == 0)
    def _():
        x = x_ref[...]
        ms = jnp.mean(x * x, axis=-1, keepdims=True)
        xn_ref[...] = (x * lax.rsqrt(ms + NORM_EPS) * g_ref[...]).astype(BF16)

    o_ref[...] = _dot(xn_ref[...], w_ref[...])


def _in_proj(x, gain, w_all, tm=1024, tn=1024):
    t = x.shape[0]
    return pl.pallas_call(
        _in_proj_kernel,
        grid=(t // tm, P_COLS // tn),
        in_specs=[
            pl.BlockSpec((tm, D), lambda i, j: (i, 0)),
            pl.BlockSpec((1, D), lambda i, j: (0, 0)),
            pl.BlockSpec((D, tn), lambda i, j: (0, j)),
        ],
        out_specs=pl.BlockSpec((tm, tn), lambda i, j: (i, j)),
        out_shape=jax.ShapeDtypeStruct((t, P_COLS), F32),
        scratch_shapes=[pltpu.VMEM((tm, D), BF16)],
        compiler_params=pltpu.CompilerParams(
            dimension_semantics=("arbitrary", "arbitrary"), vmem_limit_bytes=VMEM_LIMIT),
        name="in_proj",
    )(x, gain, w_all)


def _rwkv_stream(pr_ref, pk_ref, pv_ref, plora_ref, ga_ref, rows_ref, mixl_ref, w2_ref, a2_ref,
                 g2_ref, buf_ref, h_ref, shared, *, n_chunks, n_pairs):
    c2 = 2 * CHUNK
    tb = n_chunks * CHUNK
    width = n_pairs * LANES
    lo_k, lo_v, lo_l = width, 2 * width, 3 * width

    @pl.when(pl.program_id(1) == 0)
    def _():
        buf_ref[0:SUBLANES, :] = jnp.zeros((SUBLANES, lo_l + LORA_W), F32)
        h_ref[...] = jnp.zeros((n_pairs, LANES, LANES), F32)

    cur = jnp.concatenate([pr_ref[...], pk_ref[...], pv_ref[...], plora_ref[...]], axis=1)
    prev = pltpu.roll(jnp.concatenate([buf_ref[...], cur], axis=0), 1, 0)[SUBLANES:, :]
    buf_ref[...] = cur[tb - SUBLANES:tb, :]
    par = rows_ref[...]
    prow = lambda j: par[j:j + 1, :]
    mix = jnp.concatenate([prow(ROW_MIX_R), prow(ROW_MIX_K), prow(ROW_MIX_V), mixl_ref[...]], axis=1)
    xs = cur + (prev - cur) * mix
    r, k, v = xs[:, 0:lo_k], xs[:, lo_k:lo_v], xs[:, lo_v:lo_l]
    lw_in, la_in, lg_in = xs[:, lo_l:lo_l + 128], xs[:, lo_l + 128:lo_l + 256], xs[:, lo_l + 256:]

    w_log = -jax.nn.softplus(-(prow(ROW_W0) + _dot(jnp.tanh(lw_in), w2_ref[...]))) - 0.5
    lw_all = -jnp.exp(w_log)
    a = jax.nn.sigmoid(prow(ROW_A0) + _dot(la_in, a2_ref[...]))
    gate = _dot(jax.nn.sigmoid(lg_in), g2_ref[...])
    kk = k * prow(ROW_KK)
    kk = kk * lax.rsqrt(_head_sums(kk * kk, 2) + 1e-6)
    k = k * (1.0 + (a - 1.0) * prow(ROW_KA))
    an_all = -kk
    b_all = kk * a
    cw_all = _chunk_cumsum(lw_all)
    yield

    lane = _iota2((CHUNK, LANES), 1)
    head0 = lane < RWKV_N
    row = _iota2((c2, c2), 0)
    col = _iota2((c2, c2), 1)
    same = (row // CHUNK) == (col // CHUNK)
    strict = same & (row > col)
    incl = same & (row >= col)

    def stack(t):
        return jnp.concatenate([jnp.where(head0, t, 0.0), jnp.where(head0, 0.0, t)], axis=0)

    units = [(c, j) for c in range(n_chunks) for j in range(n_pairs)]
    us = range(len(units))
    cut = lambda t, c, j: t[c * CHUNK:(c + 1) * CHUNK, j * LANES:(j + 1) * LANES]

    at, rt, bt, kt, bd_t, kd_t, v2, w_col = [], [], [], [], [], [], [], []
    for c, j in units:
        cw = cut(cw_all, c, j)
        last = cw[CHUNK - 1:CHUNK, :]
        e_out = jnp.exp(-cw)
        e_end = jnp.exp(last - cw)
        at.append(stack(cut(an_all, c, j) * jnp.exp(cw - cut(lw_all, c, j))))
        rt.append(stack(cut(r, c, j) * jnp.exp(cw)))
        bt.append(stack(cut(b_all, c, j) * e_out))
        kt.append(stack(cut(k, c, j) * e_out))
        bd_t.append(jnp.transpose(stack(cut(b_all, c, j) * e_end)))
        kd_t.append(jnp.transpose(stack(cut(k, c, j) * e_end)))
        v2.append(stack(cut(v, c, j)))
        w_col.append(jnp.transpose(jnp.broadcast_to(jnp.exp(last), (LANES, LANES))))
        yield
    sc = []
    for u in us:
        sc.append(_dot_nt(jnp.concatenate([at[u], rt[u]], axis=0),
                          jnp.concatenate([bt[u], kt[u]], axis=0)))
        yield
    a_ak = [jnp.where(strict, s[0:c2, c2:2 * c2], 0.0) for s in sc]
    a_rb = [jnp.where(incl, s[c2:2 * c2, 0:c2], 0.0) for s in sc]
    a_rk = [jnp.where(incl, s[c2:2 * c2, c2:2 * c2], 0.0) for s in sc]
    tinv = yield from _unit_lower_inverses(
        [jnp.where(strict, -s[0:c2, 0:c2], 0.0) for s in sc], CHUNK)
    xv, pu, xp = [], [], []
    for u in us:
        xv.append(_dot(jnp.concatenate([a_ak[u], a_rk[u], kd_t[u]], axis=0), v2[u]))
        yield
    for u in us:
        pu.append(_dot(tinv[u], jnp.concatenate([at[u], xv[u][0:c2, :]], axis=1)))
        yield
    for u in us:
        xp.append(_dot(jnp.concatenate([bd_t[u], a_rb[u]], axis=0), pu[u]))
        yield
    n_add = [xp[u][0:c2, LANES:] + xv[u][2 * c2:3 * c2, :] for u in us]
    y0 = [xp[u][c2:2 * c2, LANES:] + xv[u][c2:2 * c2, :] for u in us]
    lhs_h = [jnp.concatenate([xp[u][0:c2, 0:LANES], rt[u] + xp[u][c2:2 * c2, 0:LANES]], axis=0)
             for u in us]

    h = [h_ref[j] for j in range(n_pairs)]
    ys = [[] for _ in range(n_pairs)]
    for u, (c, j) in enumerate(units):
        xh = _dot(lhs_h[u], h[j])
        y2 = xh[c2:2 * c2, :] + y0[u]
        h[j] = w_col[u] * h[j] + xh[0:c2, :] + n_add[u]
        ys[j].append(y2[0:CHUNK, :] + y2[CHUNK:c2, :])
        yield
    for j in range(n_pairs):
        h_ref[j] = h[j]

    y = jnp.concatenate([jnp.concatenate(yj, axis=0) for yj in ys], axis=1)
    mu = _head_sums(y, 1) * (1.0 / RWKV_N)
    yc = y - mu
    var = _head_sums(yc * yc, 1) * (1.0 / RWKV_N)
    gn = yc * lax.rsqrt(var + RWKV_GN_EPS) * prow(ROW_LNW) + prow(ROW_LNB)
    bonus = _head_sums(r * k * prow(ROW_RK), 2) * v
    shared["ya"] = jax.nn.sigmoid(ga_ref[...]) * ((gn + bonus) * gate)


def _gdn_stream(pq_ref, pk_ref, pv_ref, ba_ref, z_ref, gb_ref, cq_ref, ck_ref, cv_ref,
                alog_ref, dtb_ref, nw_ref, o_ref, buf_ref, s_ref, shared, *, n_chunks, n_heads):
    tb = n_chunks * CHUNK
    width = n_heads * LANES
    first_head = pl.program_id(0) * n_heads

    @pl.when(pl.program_id(1) == 0)
    def _():
        buf_ref[0:SUBLANES, :] = jnp.zeros((SUBLANES, 3 * width), F32)
        s_ref[...] = jnp.zeros((n_heads, GDN_N, GDN_N), F32)

    xs = jnp.concatenate(
        [buf_ref[...], jnp.concatenate([pq_ref[...], pk_ref[...], pv_ref[...]], axis=1)], axis=0)
    buf_ref[...] = xs[tb:tb + SUBLANES, :]
    cw = jnp.concatenate([cq_ref[...], ck_ref[...], cv_ref[...]], axis=1)
    acc = xs[SUBLANES:, :] * cw[GDN_CONV - 1:GDN_CONV, :]
    for j in range(GDN_CONV - 2, -1, -1):
        xs = pltpu.roll(xs, 1, 0)
        acc = acc + xs[SUBLANES:, :] * cw[j:j + 1, :]
    qkv = acc * jax.nn.sigmoid(acc)
    yield

    def l2n(t):
        return t * lax.rsqrt(jnp.sum(t * t, axis=-1, keepdims=True) + 1e-6)

    ba = ba_ref[...]
    beta_in = jax.nn.sigmoid(ba)
    g_in = -jnp.exp(alog_ref[...]) * jax.nn.softplus(ba + dtb_ref[...])
    sr = _iota2((LANES, LANES), 0)
    q_h, k_h, kb_h, vb_h, gc_h = [], [], [], [], []
    for j in range(n_heads):
        lanes = slice(j * LANES, (j + 1) * LANES)
        head = first_head + j
        beta = _dot01_right(beta_in, jnp.where(sr == head, 1.0, 0.0), 2)
        g = _dot01_right(g_in, jnp.where(sr == head + GDN_HEADS, 1.0, 0.0), 3)
        k_all = l2n(qkv[:, width:2 * width][:, lanes])
        q_h.append(l2n(qkv[:, 0:width][:, lanes]) * (GDN_N ** -0.5))
        k_h.append(k_all)
        kb_h.append(k_all * beta)
        vb_h.append(qkv[:, 2 * width:3 * width][:, lanes] * beta)
        gc_h.append(_chunk_cumsum(g))
        yield

    tr = _iota2((CHUNK, CHUNK), 0)
    tc = _iota2((CHUNK, CHUNK), 1)
    causal = tr >= tc
    strict = tr > tc
    units = [(c, j) for c in range(n_chunks) for j in range(n_heads)]
    us = range(len(units))
    cut = lambda per_head, c, j: per_head[j][c * CHUNK:(c + 1) * CHUNK, :]

    q = [cut(q_h, c, j) for c, j in units]
    k = [cut(k_h, c, j) for c, j in units]
    kb = [cut(kb_h, c, j) for c, j in units]
    vb = [cut(vb_h, c, j) for c, j in units]
    gcb = [cut(gc_h, c, j) for c, j in units]
    decay = []
    for u in us:
        g_row = jnp.transpose(gcb[u])[0:CHUNK, :]
        dlog = gcb[u][:, 0:CHUNK] - g_row
        decay.append(jnp.where(causal, jnp.exp(jnp.minimum(dlog, 0.0)), 0.0))
        yield
    sc = []
    for u in us:
        sc.append(_dot_nt(jnp.concatenate([kb[u], q[u]], axis=0), k[u]))
        yield
    qk = [sc[u][CHUNK:2 * CHUNK, :] * decay[u] for u in us]
    tinv = yield from _unit_lower_inverses(
        [jnp.where(strict, sc[u][0:CHUNK, :] * decay[u], 0.0) for u in us], CHUNK)
    e_gc = [jnp.exp(gg) for gg in gcb]
    g_last = [gg[CHUNK - 1:CHUNK, :] for gg in gcb]
    uw, mn, qq = [], [], []
    for u in us:
        uw.append(_dot(tinv[u], jnp.concatenate([vb[u], kb[u] * e_gc[u]], axis=1)))
        yield
    for u in us:
        kd_t = jnp.transpose(k[u] * jnp.exp(g_last[u] - gcb[u]))
        mn.append(_dot(kd_t, uw[u]))
        yield
    for u in us:
        qq.append(_dot(qk[u], uw[u]))
        yield

    s = [s_ref[j] for j in range(n_heads)]
    os = [[] for _ in range(n_heads)]
    for u, (c, j) in enumerate(units):
        os[j].append(_dot(q[u] * e_gc[u] - qq[u][:, LANES:], s[j]) + qq[u][:, 0:LANES])
        s[j] = jnp.exp(g_last[u]) * s[j] - _dot(mn[u][:, LANES:], s[j]) + mn[u][:, 0:LANES]
        yield
    for j in range(n_heads):
        s_ref[j] = s[j]

    z = z_ref[...]
    gated = z * jax.nn.sigmoid(z) * jax.nn.sigmoid(gb_ref[...])
    while "ya" not in shared:
        yield
    ya = shared["ya"]
    for j in range(n_heads):
        lanes = slice(j * LANES, (j + 1) * LANES)
        o = jnp.concatenate(os[j], axis=0)
        ms = jnp.mean(o * o, axis=-1, keepdims=True)
        y_b = o * lax.rsqrt(ms + NORM_EPS) * nw_ref[...] * gated[:, lanes]
        o_ref[:, lanes] = (ya[:, lanes] + y_b).astype(BF16)


def _rwkv_mix_kernel(*refs, n_chunks, n_pairs):
    *in_refs, o_ref, buf_ref, h_ref = refs
    shared = {}
    _interleave(_rwkv_stream(*in_refs, buf_ref, h_ref, shared, n_chunks=n_chunks, n_pairs=n_pairs))
    o_ref[...] = shared["ya"]


def _rwkv_mix(p, rows, mix_lora, w2, a2, g2, tb=512, n_pairs=4):
    t = p.shape[0]
    width = n_pairs * LANES
    pcol = lambda base: pl.BlockSpec((tb, width), lambda h, i: (i, base // width + h))
    wcol = lambda height: pl.BlockSpec((height, width), lambda h, i: (0, h))
    return pl.pallas_call(
        functools.partial(_rwkv_mix_kernel, n_chunks=tb // CHUNK, n_pairs=n_pairs),
        grid=(D // width, t // tb),
        in_specs=[pcol(C_R), pcol(C_K), pcol(C_V),
                  pl.BlockSpec((tb, LORA_W), lambda h, i: (i, C_LORA // LORA_W)),
                  pcol(C_GA), wcol(RWKV_ROWS),
                  pl.BlockSpec((1, LORA_W), lambda h, i: (0, 0)),
                  wcol(128), wcol(128), wcol(256)],
        out_specs=pl.BlockSpec((tb, width), lambda h, i: (i, h)),
        out_shape=jax.ShapeDtypeStruct((t, D), F32),
        scratch_shapes=[pltpu.VMEM((SUBLANES, 3 * width + LORA_W), F32),
                        pltpu.VMEM((n_pairs, LANES, LANES), F32)],
        compiler_params=pltpu.CompilerParams(
            dimension_semantics=("arbitrary", "arbitrary"), vmem_limit_bytes=VMEM_LIMIT),
        name="rwkv_mix",
    )(p, p, p, p, p, rows, mix_lora, w2, a2, g2)


def _gdn_mix_kernel(*refs, n_chunks, n_heads):
    *in_refs, ya_ref, o_ref, buf_ref, s_ref = refs
    shared = {"ya": ya_ref[...]}
    _interleave(_gdn_stream(*in_refs, o_ref, buf_ref, s_ref, shared, n_chunks=n_chunks,
                            n_heads=n_heads))


def _gdn_mix(p, y_a, conv_w, a_log, dt_bias, norm_w, tb=512, n_heads=4):
    t = p.shape[0]
    width = n_heads * LANES
    pcol = lambda base: pl.BlockSpec((tb, width), lambda h, i: (i, base // width + h))
    ccol = lambda base: pl.BlockSpec((GDN_CONV, width), lambda h, i: (0, base // width + h))
    one = pl.BlockSpec((1, LANES), lambda h, i: (0, 0))
    return pl.pallas_call(
        functools.partial(_gdn_mix_kernel, n_chunks=tb // CHUNK, n_heads=n_heads),
        grid=(D // width, t // tb),
        in_specs=[pcol(C_QKV), pcol(C_QKV + D), pcol(C_QKV + 2 * D),
                  pl.BlockSpec((tb, LANES), lambda h, i: (i, C_BA // LANES)),
                  pcol(C_Z), pcol(C_GB), ccol(0), ccol(D), ccol(2 * D), one, one, one,
                  pl.BlockSpec((tb, width), lambda h, i: (i, h))],
        out_specs=pl.BlockSpec((tb, width), lambda h, i: (i, h)),
        out_shape=jax.ShapeDtypeStruct((t, D), BF16),
        scratch_shapes=[pltpu.VMEM((SUBLANES, 3 * width), F32),
                        pltpu.VMEM((n_heads, GDN_N, GDN_N), F32)],
        compiler_params=pltpu.CompilerParams(
            dimension_semantics=("arbitrary", "arbitrary"), vmem_limit_bytes=VMEM_LIMIT),
        name="gdn_mix",
    )(p, p, p, p, p, p, conv_w, conv_w, conv_w, a_log, dt_bias, norm_w, y_a)


def _out_proj_kernel(mixed_ref, x_ref, w_ref, g_ref, o_ref):
    y = _dot(mixed_ref[...], w_ref[...])
    ms = jnp.mean(y * y, axis=-1, keepdims=True)
    o_ref[...] = x_ref[...] + y * lax.rsqrt(ms + NORM_EPS) * g_ref[...]


def _out_proj(mixed, x, w_out, gain, tm=512):
    t = x.shape[0]
    blk = pl.BlockSpec((tm, D), lambda i: (i, 0))
    return pl.pallas_call(
        _out_proj_kernel,
        grid=(t // tm,),
        in_specs=[blk, blk,
                  pl.BlockSpec((D, D), lambda i: (0, 0)),
                  pl.BlockSpec((1, D), lambda i: (0, 0))],
        out_specs=blk,
        out_shape=jax.ShapeDtypeStruct((t, D), F32),
        compiler_params=pltpu.CompilerParams(
            dimension_semantics=("arbitrary",), vmem_limit_bytes=VMEM_LIMIT),
        name="out_proj",
    )(mixed, x, w_out, gain)


def _ffn_kernel(x_ref, gpre_ref, wg_ref, wu_ref, wd_ref, gpost_ref, o_ref, xn_ref, acc_ref):
    f = pl.program_id(1)

    @pl.when(f == 0)
    def _():
        x = x_ref[...]
        ms = jnp.mean(x * x, axis=-1, keepdims=True)
        xn_ref[...] = (x * lax.rsqrt(ms + NORM_EPS) * gpre_ref[...]).astype(BF16)
        acc_ref[...] = jnp.zeros(acc_ref.shape, F32)

    xn = xn_ref[...]
    gate = _dot(xn, wg_ref[...])
    up = _dot(xn, wu_ref[...])
    hidden = (gate * jax.nn.sigmoid(gate) * up).astype(BF16)
    acc_ref[...] += _dot(hidden, wd_ref[...])

    @pl.when(f == pl.num_programs(1) - 1)
    def _():
        y = acc_ref[...]
        ms = jnp.mean(y * y, axis=-1, keepdims=True)
        o_ref[...] = x_ref[...] + y * lax.rsqrt(ms + NORM_EPS) * gpost_ref[...]


def _ffn(x, g_pre, w_gate, w_up, w_down, g_post, tm=512, tf=512):
    t = x.shape[0]
    return pl.pallas_call(
        _ffn_kernel,
        grid=(t // tm, D_FF // tf),
        in_specs=[pl.BlockSpec((tm, D), lambda i, f: (i, 0)),
                  pl.BlockSpec((1, D), lambda i, f: (0, 0)),
                  pl.BlockSpec((D, tf), lambda i, f: (0, f)),
                  pl.BlockSpec((D, tf), lambda i, f: (0, f)),
                  pl.BlockSpec((tf, D), lambda i, f: (f, 0)),
                  pl.BlockSpec((1, D), lambda i, f: (0, 0))],
        out_specs=pl.BlockSpec((tm, D), lambda i, f: (i, 0)),
        out_shape=jax.ShapeDtypeStruct((t, D), F32),
        scratch_shapes=[pltpu.VMEM((tm, D), BF16), pltpu.VMEM((tm, D), F32)],
        compiler_params=pltpu.CompilerParams(
            dimension_semantics=("arbitrary", "arbitrary"), vmem_limit_bytes=VMEM_LIMIT),
        name="ffn",
    )(x, g_pre, w_gate, w_up, w_down, g_post)


def _pad_cols(w, width):
    return jnp.pad(w, ((0, 0), (0, width - w.shape[1])))


def _pad_rows(w, height):
    return jnp.pad(w, ((0, height - w.shape[0]), (0, 0)))


def _layer(x, norm_mix_pre, norm_mix_post, norm_ffn_pre, norm_ffn_post, w_in, rwkv_mix, rwkv_w0,
           rwkv_w2, rwkv_a0, rwkv_a2, rwkv_g2, rwkv_k_k, rwkv_k_a, rwkv_r_k, rwkv_ln_w, rwkv_ln_b,
           gdn_conv_w, gdn_a_log, gdn_dt_bias, gdn_norm_w, w_out, ffn_w_gate, ffn_w_up, ffn_w_down):
    row = lambda v: v.reshape(1, -1).astype(F32)
    o_lw = 3 * D
    o_la = o_lw + 96
    o_lg = o_la + 96
    o_qkv = o_lg + 256
    o_z = o_qkv + 3 * D
    o_beta = o_z + D
    o_ga = o_beta + 2 * GDN_HEADS
    o_gb = o_ga + D
    cols = lambda lo, hi: w_in[:, lo:hi].astype(BF16)
    w_all = jnp.concatenate([
        cols(0, 3 * D), cols(o_qkv, o_z), cols(o_z, o_beta), cols(o_ga, o_gb), cols(o_gb, o_gb + D),
        _pad_cols(cols(o_lw, o_la), 128), _pad_cols(cols(o_la, o_lg), 128), cols(o_lg, o_qkv),
        _pad_cols(cols(o_beta, o_ga), P_COLS - C_BA),
    ], axis=1)
    mix_lora = jnp.concatenate([
        _pad_cols(row(rwkv_mix[o_lw:o_la]), 128), _pad_cols(row(rwkv_mix[o_la:o_lg]), 128),
        row(rwkv_mix[o_lg:o_qkv])], axis=1)
    rwkv_rows = _pad_rows(jnp.concatenate([
        row(rwkv_mix[0:D]), row(rwkv_mix[D:2 * D]), row(rwkv_mix[2 * D:3 * D]), row(rwkv_w0),
        row(rwkv_a0), row(rwkv_k_k), row(rwkv_k_a), row(rwkv_r_k), row(rwkv_ln_w), row(rwkv_ln_b),
    ], axis=0), RWKV_ROWS)
    lane_pad = lambda v: jnp.pad(row(v), ((0, 0), (GDN_HEADS, LANES - 2 * GDN_HEADS)))

    p = _in_proj(x, row(norm_mix_pre), w_all)
    y_a = _rwkv_mix(p, rwkv_rows, mix_lora, _pad_rows(rwkv_w2, 128).astype(BF16),
                    _pad_rows(rwkv_a2, 128).astype(BF16), rwkv_g2.astype(BF16))
    mixed = _gdn_mix(p, y_a, gdn_conv_w, lane_pad(gdn_a_log), lane_pad(gdn_dt_bias),
                     row(gdn_norm_w))
    x1 = _out_proj(mixed, x, w_out.astype(BF16), row(norm_mix_post))
    return _ffn(x1, row(norm_ffn_pre), ffn_w_gate.astype(BF16), ffn_w_up.astype(BF16),
                ffn_w_down.astype(BF16), row(norm_ffn_post))


def kernel(x, norm_mix_pre, norm_mix_post, norm_ffn_pre, norm_ffn_post, w_in, rwkv_mix, rwkv_w0,
           rwkv_w2, rwkv_a0, rwkv_a2, rwkv_g2, rwkv_k_k, rwkv_k_a, rwkv_r_k, rwkv_ln_w, rwkv_ln_b,
           gdn_conv_w, gdn_a_log, gdn_dt_bias, gdn_norm_w, w_out, ffn_w_gate, ffn_w_up, ffn_w_down):
    bsz, seq, d = x.shape
    assert (bsz, seq, d) == (1, SEQ, D)
    h = x[0]
    weights = (norm_mix_pre, norm_mix_post, norm_ffn_pre, norm_ffn_post, w_in, rwkv_mix, rwkv_w0,
               rwkv_w2, rwkv_a0, rwkv_a2, rwkv_g2, rwkv_k_k, rwkv_k_a, rwkv_r_k, rwkv_ln_w,
               rwkv_ln_b, gdn_conv_w, gdn_a_log, gdn_dt_bias, gdn_norm_w, w_out, ffn_w_gate,
               ffn_w_up, ffn_w_down)
    for layer in range(norm_mix_pre.shape[0]):
        h = _layer(h, *(w[layer] for w in weights))
    return h[None]
```
